```python
import jax
import jax.numpy as jnp
from jax import lax
import numpy as np

D_MODEL = 2048
BATCH = 1
SEQ = 16384
DEPTH = 1

N_HEADS = 16
N_KV_GROUPS = 2
HEADS_PER_GROUP = N_HEADS // N_KV_GROUPS
HEAD_DIM = 128
CMP_BLOCK = 32
CMP_STRIDE = 16
CMP_HIDDEN = 2 * HEAD_DIM
SEL_BLOCK = 64
SEL_TOP_N = 16
WINDOW = 512
Q_BLOCK = SEL_BLOCK
FORCE_BONUS = 1e6
CONV_CH = D_MODEL
CONV_WIDTH = 31
FFN_HIDDEN = 256 * ((8 * D_MODEL // 3 + 255) // 256)
FFN_CONV_WIDTH = 3
ADA_SCALE = 0.5
EPS = 1e-6
NEG_INF = -1e30

Q_WIDTH = N_HEADS * HEAD_DIM
KV_WIDTH = N_KV_GROUPS * HEAD_DIM
IN_PROJ_WIDTHS = (Q_WIDTH,) + (KV_WIDTH,) * 6 + (3 * N_HEADS, 2 * CONV_CH, 2 * D_MODEL)
IN_PROJ_WIDTH = sum(IN_PROJ_WIDTHS)

kernel_name = 'nsa_conformer_convffn_hybrid_block'


def rmsnorm(x, g):
    xf = x.astype(jnp.float32)
    y = xf * lax.rsqrt(jnp.mean(xf * xf, axis=-1, keepdims=True) + EPS)
    return (y * g.astype(jnp.float32)).astype(x.dtype)


def layernorm(x, g, b):
    xf = x.astype(jnp.float32)
    mu = jnp.mean(xf, axis=-1, keepdims=True)
    var = jnp.mean(jnp.square(xf - mu), axis=-1, keepdims=True)
    y = (xf - mu) * lax.rsqrt(var + EPS) * g.astype(jnp.float32) + b.astype(jnp.float32)
    return y.astype(x.dtype)


def masked_softmax(s, mask):
    s = jnp.where(mask, s.astype(jnp.float32), NEG_INF)
    m = jnp.max(s, axis=-1, keepdims=True)
    e = jnp.where(mask, jnp.exp(s - m), 0.0)
    return e / jnp.maximum(jnp.sum(e, axis=-1, keepdims=True), 1e-30)


def causal_dwconv(x, w, b):
    k = w.shape[0]
    y = lax.conv_general_dilated(x, w[:, None, :], window_strides=(1,), padding=[(k - 1, 0)],
                                 dimension_numbers=('NWC', 'WIO', 'NWC'),
                                 feature_group_count=x.shape[-1])
    return y + b


def compress(k_raw, pe, w1, w2):
    B, T, G, dk = k_raw.shape
    r = CMP_BLOCK // CMP_STRIDE
    n_chunks = T // CMP_STRIDE
    n_cmp = n_chunks - r + 1
    chunks = k_raw.reshape(B, n_chunks, CMP_STRIDE, G, dk)
    blocks = jnp.concatenate([chunks[:, j:j + n_cmp] for j in range(r)], axis=2)
    blocks = blocks + pe[None, None, :, None, :]
    flat = blocks.transpose(0, 3, 1, 2, 4).reshape(B, G, n_cmp, CMP_BLOCK * dk)
    return jax.nn.silu(flat @ w1) @ w2


def nsa_attention(q, k_cmp, v_cmp, k_slc, v_slc, k_win, v_win, gate_logits,
                  cmp_pe, w_kc1, w_kc2, w_vc1, w_vc2):
    B, T = q.shape[0], q.shape[1]
    G, Hg, dk = N_KV_GROUPS, HEADS_PER_GROUP, HEAD_DIM
    f32 = jnp.float32
    scale = dk ** -0.5
    slopes = jnp.exp2(-8.0 * jnp.arange(1, N_HEADS + 1, dtype=f32) / N_HEADS).reshape(G, Hg)
    slopes = slopes[None, :, :, None, None]

    kc = compress(k_cmp, cmp_pe, w_kc1, w_kc2)
    vc = compress(v_cmp, cmp_pe, w_vc1, w_vc2)
    n_cmp = kc.shape[2]
    cmp_start = jnp.arange(n_cmp, dtype=jnp.int32) * CMP_STRIDE
    pos_cmp = cmp_start + CMP_BLOCK - 1

    n_sel = T // SEL_BLOCK
    top_n = min(SEL_TOP_N, n_sel)
    sel_start = jnp.arange(n_sel, dtype=jnp.int32) * SEL_BLOCK
    overlap = ((cmp_start[:, None] < sel_start[None, :] + SEL_BLOCK)
               & (cmp_start[:, None] + CMP_BLOCK > sel_start[None, :])).astype(f32)
    ks_blk = k_slc.reshape(B, n_sel, SEL_BLOCK, G, dk).transpose(0, 3, 1, 2, 4)
    vs_blk = v_slc.reshape(B, n_sel, SEL_BLOCK, G, dk).transpose(0, 3, 1, 2, 4)
    kw_pad = jnp.pad(k_win, ((0, 0), (WINDOW, 0), (0, 0), (0, 0)))
    vw_pad = jnp.pad(v_win, ((0, 0), (WINDOW, 0), (0, 0), (0, 0)))

    n_qb = T // Q_BLOCK
    q_blocks = q.reshape(B, n_qb, Q_BLOCK, G, Hg, dk).transpose(1, 0, 3, 4, 2, 5)
    g_blocks = jax.nn.sigmoid(gate_logits.astype(f32)).reshape(
        B, n_qb, Q_BLOCK, 3, G, Hg).transpose(1, 0, 3, 4, 5, 2)
    gather = jax.vmap(jax.vmap(lambda blk, idx: blk[idx]))
    sel_off = jnp.arange(SEL_BLOCK, dtype=jnp.int32)
    win_off = jnp.arange(WINDOW + Q_BLOCK, dtype=jnp.int32)
    blk_ids = jnp.arange(n_sel, dtype=jnp.int32)

    def block_step(args):
        qb, qblk, gblk = args
        t0 = qb * Q_BLOCK
        t = t0 + jnp.arange(Q_BLOCK, dtype=jnp.int32)
        tf = t.astype(f32)

        s = jnp.einsum('bghqd,bgnd->bghqn', qblk, kc).astype(f32) * scale
        s = s - slopes * (tf[:, None] - pos_cmp.astype(f32)[None, :])
        p_cmp = masked_softmax(s, pos_cmp[None, :] <= t[:, None])
        o_cmp = jnp.einsum('bghqn,bgnd->bghqd', p_cmp.astype(vc.dtype), vc).astype(f32)

        imp = jnp.einsum('bghqn,ns->bgqs', p_cmp, overlap)
        cur = t // SEL_BLOCK
        valid = blk_ids[None, :] <= cur[:, None]
        forced = ((blk_ids[None, :] == 0) | (blk_ids[None, :] == cur[:, None])
                  | (blk_ids[None, :] == cur[:, None] - 1))
        score = jnp.where(valid, imp + FORCE_BONUS * forced.astype(f32), -1.0)
        _, idx = lax.top_k(score, top_n)
        k_sel = gather(ks_blk, idx).reshape(B, G, Q_BLOCK, top_n * SEL_BLOCK, dk)
        v_sel = gather(vs_blk, idx).reshape(B, G, Q_BLOCK, top_n * SEL_BLOCK, dk)
        pos_sel = (idx[..., None] * SEL_BLOCK + sel_off).reshape(B, G, Q_BLOCK, top_n * SEL_BLOCK)
        pos_sel = pos_sel[:, :, None]
        s = jnp.einsum('bghqd,bgqkd->bghqk', qblk, k_sel).astype(f32) * scale
        s = s - slopes * (tf[:, None] - pos_sel.astype(f32))
        p = masked_softmax(s, pos_sel <= t[:, None])
        o_slc = jnp.einsum('bghqk,bgqkd->bghqd', p.astype(v_sel.dtype), v_sel).astype(f32)

        k_w = lax.dynamic_slice_in_dim(kw_pad, t0, WINDOW + Q_BLOCK, axis=1)
        v_w = lax.dynamic_slice_in_dim(vw_pad, t0, WINDOW + Q_BLOCK, axis=1)
        pos_w = t0 - WINDOW + win_off
        dist = t[:, None] - pos_w[None, :]
        mask_w = (dist >= 0) & (dist < WINDOW) & (pos_w[None, :] >= 0)
        s = jnp.einsum('bghqd,bkgd->bghqk', qblk, k_w).astype(f32) * scale
        s = s - slopes * dist.astype(f32)
        p = masked_softmax(s, mask_w)
        o_win = jnp.einsum('bghqk,bkgd->bghqd', p.astype(v_w.dtype), v_w).astype(f32)

        out = (gblk[:, 0, ..., None] * o_cmp + gblk[:, 1, ..., None] * o_slc
               + gblk[:, 2, ..., None] * o_win)
        return out.astype(q.dtype)

    o = lax.map(block_step, (jnp.arange(n_qb, dtype=jnp.int32), q_blocks, g_blocks))
    return o.transpose(1, 0, 4, 2, 3, 5).reshape(B, T, N_HEADS * dk)


def conformer_conv(glu_in, w_dw, b_dw, ln_g, ln_b, w_pw, b_pw):
    a, g = jnp.split(glu_in, 2, axis=-1)
    u = a * jax.nn.sigmoid(g)
    u = causal_dwconv(u, w_dw, b_dw)
    u = jax.nn.silu(layernorm(u, ln_g, ln_b))
    return u @ w_pw + b_pw


def conv_ffn(h, w_up, w_dw, b_dw, w_down):
    u = causal_dwconv(h @ w_up, w_dw, b_dw)
    a, g = jnp.split(u, 2, axis=-1)
    return (jax.nn.silu(g) * a) @ w_down


def setup_inputs(seed: int = 0) -> dict:
    key = jax.random.key(seed)
    ks = jax.random.split(key, 25)
    L, D = DEPTH, D_MODEL

    def nrm(k, shape, s):
        return jax.random.normal(k, shape, jnp.float32) * s

    return {
        'x': nrm(ks[0], (BATCH, SEQ, D), 1.0),
        'c': nrm(ks[1], (BATCH, D), 1.0),
        'w_ada': nrm(ks[2], (L, D, 6 * D), ADA_SCALE * D ** -0.5),
        'b_ada': nrm(ks[3], (L, 6 * D), 0.01),
        'norm1_g': 1.0 + nrm(ks[4], (L, D), 0.02),
        'w_in': nrm(ks[5], (L, D, IN_PROJ_WIDTH), D ** -0.5),
        'cmp_pe': nrm(ks[6], (L, CMP_BLOCK, HEAD_DIM), 0.1),
        'w_kc1': nrm(ks[7], (L, CMP_BLOCK * HEAD_DIM, CMP_HIDDEN), (CMP_BLOCK * HEAD_DIM) ** -0.5),
        'w_kc2': nrm(ks[8], (L, CMP_HIDDEN, HEAD_DIM), CMP_HIDDEN ** -0.5),
        'w_vc1': nrm(ks[9], (L, CMP_BLOCK * HEAD_DIM, CMP_HIDDEN), (CMP_BLOCK * HEAD_DIM) ** -0.5),
        'w_vc2': nrm(ks[10], (L, CMP_HIDDEN, HEAD_DIM), CMP_HIDDEN ** -0.5),
        'w_o_nsa': nrm(ks[11], (L, Q_WIDTH, D), Q_WIDTH ** -0.5),
        'conv_dw_w': nrm(ks[12], (L, CONV_WIDTH, CONV_CH), CONV_WIDTH ** -0.5),
        'conv_dw_b': nrm(ks[13], (L, CONV_CH), 0.01),
        'conv_ln_g': 1.0 + nrm(ks[14], (L, CONV_CH), 0.02),
        'conv_ln_b': nrm(ks[15], (L, CONV_CH), 0.01),
        'conv_pw_w': nrm(ks[16], (L, CONV_CH, D), CONV_CH ** -0.5),
        'conv_pw_b': nrm(ks[17], (L, D), 0.01),
        'w_out': nrm(ks[18], (L, D, D), D ** -0.5),
        'norm2_g': 1.0 + nrm(ks[19], (L, D), 0.02),
        'ffn_w_up': nrm(ks[20], (L, D, 2 * FFN_HIDDEN), D ** -0.5),
        'ffn_dw_w': nrm(ks[21], (L, FFN_CONV_WIDTH, 2 * FFN_HIDDEN), FFN_CONV_WIDTH ** -0.5),
        'ffn_dw_b': nrm(ks[22], (L, 2 * FFN_HIDDEN), 0.01),
        'ffn_w_down': nrm(ks[23], (L, FFN_HIDDEN, D), FFN_HIDDEN ** -0.5),
        'final_g': 1.0 + nrm(ks[24], (D,), 0.02),
    }


def reference(x, c, w_ada, b_ada, norm1_g, w_in, cmp_pe, w_kc1, w_kc2, w_vc1, w_vc2,
              w_o_nsa, conv_dw_w, conv_dw_b, conv_ln_g, conv_ln_b, conv_pw_w, conv_pw_b,
              w_out, norm2_g, ffn_w_up, ffn_dw_w, ffn_dw_b, ffn_w_down, final_g):
    B, T = x.shape[0], x.shape[1]
    split_points = np.cumsum(IN_PROJ_WIDTHS)[:-1].tolist()
    c_act = jax.nn.silu(c)
    for l in range(DEPTH):
        ada = (c_act @ w_ada[l] + b_ada[l])[:, None, :]
        sh1, sc1, g1, sh2, sc2, g2 = jnp.split(ada, 6, axis=-1)

        h = rmsnorm(x, norm1_g[l]) * (1.0 + sc1) + sh1
        proj = h @ w_in[l]
        q, kc, vc, ksl, vsl, kw, vw, g_nsa, glu_in, g_merge = jnp.split(proj, split_points, axis=-1)
        q = q.reshape(B, T, N_HEADS, HEAD_DIM)
        kv = [a.reshape(B, T, N_KV_GROUPS, HEAD_DIM) for a in (kc, vc, ksl, vsl, kw, vw)]
        o_nsa = nsa_attention(q, kv[0], kv[1], kv[2], kv[3], kv[4], kv[5],
                              g_nsa.reshape(B, T, 3, N_HEADS),
                              cmp_pe[l], w_kc1[l], w_kc2[l], w_vc1[l], w_vc2[l])
        y_a = o_nsa @ w_o_nsa[l]
        y_b = conformer_conv(glu_in, conv_dw_w[l], conv_dw_b[l], conv_ln_g[l], conv_ln_b[l],
                             conv_pw_w[l], conv_pw_b[l])
        ga, gb = jnp.split(g_merge, 2, axis=-1)
        merged = jax.nn.sigmoid(ga) * y_a + jax.nn.sigmoid(gb) * y_b
        x = x + g1 * (merged @ w_out[l])

        h = rmsnorm(x, norm2_g[l]) * (1.0 + sc2) + sh2
        x = x + g2 * conv_ffn(h, ffn_w_up[l], ffn_dw_w[l], ffn_dw_b[l], ffn_w_down[l])
    return rmsnorm(x, final_g)
```

```python
import functools

import jax
import jax.numpy as jnp
from jax import lax
from jax.experimental import pallas as pl
from jax.experimental.pallas import tpu as pltpu

F32 = jnp.float32
BF16 = jnp.bfloat16

N_HEADS = 16
N_GROUPS = 2
HEADS_PER_GROUP = N_HEADS // N_GROUPS
HEAD_DIM = 128
CMP_BLOCK = 32
CMP_STRIDE = 16
CMP_HIDDEN = 2 * HEAD_DIM
SEL_BLOCK = 64
SEL_TOP_N = 16
WINDOW = 512
FORCE_BONUS = 1e6
CONV_WIDTH = 31
FFN_CONV_WIDTH = 3
EPS = 1e-6
NEG_INF = -1e30

V7X_VMEM_BYTES = 64 * 1024 * 1024
VMEM_LIMIT_BYTES = V7X_VMEM_BYTES - 8 * 1024 * 1024
LANES = 128

Q_TILE = 128
SLC_CHUNK = 512
WIN_CHUNK = 128
CMP_CHUNK = 256


def _cparams(*sem):
    return pltpu.CompilerParams(dimension_semantics=sem, vmem_limit_bytes=VMEM_LIMIT_BYTES)


def _sigmoid(v):
    return 1.0 / (1.0 + jnp.exp(-v))


def _nt_dot(a, b):
    return lax.dot_general(a, b, (((1,), (1,)), ((), ())), preferred_element_type=F32)


def _ada_call(c_row, w, b):
    d, n = w.shape
    tn = 1024
    cb = jnp.broadcast_to(c_row.reshape(d, 1), (d, LANES))

    def body(cb_ref, w_ref, b_ref, o_ref):
        cv = cb_ref[...]
        act = cv * _sigmoid(cv)
        for j in range(tn // LANES):
            sl = slice(j * LANES, (j + 1) * LANES)
            row = jnp.sum(w_ref[:, sl] * act, axis=0, keepdims=True) + b_ref[:, sl]
            o_ref[:, sl] = jnp.broadcast_to(row, (8, LANES))

    return pl.pallas_call(
        body,
        grid=(n // tn,),
        in_specs=[pl.BlockSpec((d, LANES), lambda j: (0, 0)),
                  pl.BlockSpec((d, tn), lambda j: (0, j)),
                  pl.BlockSpec((1, tn), lambda j: (0, j))],
        out_specs=pl.BlockSpec((8, tn), lambda j: (0, j)),
        out_shape=jax.ShapeDtypeStruct((8, n), F32),
        compiler_params=_cparams("parallel"),
        name="ada",
    )(cb, w, b.reshape(1, n))


def _norm_call(x, gain, ada=None, shift_idx=0, scale_idx=0, out_dtype=BF16):
    t, d = x.shape
    tm = 512

    def body(*refs):
        if ada is None:
            x_ref, g_ref, o_ref = refs
        else:
            x_ref, g_ref, sh_ref, sc_ref, o_ref = refs
        xv = x_ref[...]
        y = xv * lax.rsqrt(jnp.mean(xv * xv, axis=-1, keepdims=True) + EPS) * g_ref[...]
        if ada is not None:
            y = y * (1.0 + sc_ref[0:1, :]) + sh_ref[0:1, :]
        o_ref[...] = y.astype(o_ref.dtype)

    in_specs = [pl.BlockSpec((tm, d), lambda i: (i, 0)), pl.BlockSpec((1, d), lambda i: (0, 0))]
    args = [x, gain.reshape(1, d)]
    if ada is not None:
        in_specs += [pl.BlockSpec((8, d), lambda i: (0, shift_idx)),
                     pl.BlockSpec((8, d), lambda i: (0, scale_idx))]
        args += [ada, ada]
    return pl.pallas_call(
        body,
        grid=(t // tm,),
        in_specs=in_specs,
        out_specs=pl.BlockSpec((tm, d), lambda i: (i, 0)),
        out_shape=jax.ShapeDtypeStruct((t, d), out_dtype),
        compiler_params=_cparams("parallel"),
        name="norm",
    )(*args)


def _matmul_call(a, w, *, tm, tn, out_dtype, name, epilogue=None, extras=(), extra_specs=()):
    m, k = a.shape
    n = w.shape[1]
    tm = min(tm, m)

    def body(a_ref, w_ref, *rest):
        o_ref = rest[-1]
        acc = jnp.dot(a_ref[...], w_ref[...], preferred_element_type=F32)
        if epilogue is not None:
            acc = epilogue(acc, *rest[:-1])
        o_ref[...] = acc.astype(o_ref.dtype)

    return pl.pallas_call(
        body,
        grid=(m // tm, n // tn),
        in_specs=[pl.BlockSpec((tm, k), lambda i, j: (i, 0)),
                  pl.BlockSpec((k, tn), lambda i, j: (0, j))] + list(extra_specs),
        out_specs=pl.BlockSpec((tm, tn), lambda i, j: (i, j)),
        out_shape=jax.ShapeDtypeStruct((m, n), out_dtype),
        compiler_params=_cparams("parallel", "parallel"),
        name=name,
    )(a, w, *extras)


def _nt_proj_call(h, w_vt, w_gt):
    t, d = h.shape
    tq = SLC_CHUNK
    nv = w_vt.shape[0] // 2
    ng = w_gt.shape[0]
    per = tq // WIN_CHUNK

    def body(h_ref, wv_ref, wg_ref, vs_ref, vw_ref, g_ref):
        hv = h_ref[...]
        r = _nt_dot(wv_ref[...], hv)
        vs_ref[0] = r[0:nv].astype(BF16)
        for kk in range(per):
            vw_ref[kk] = r[nv:2 * nv, kk * WIN_CHUNK:(kk + 1) * WIN_CHUNK].astype(BF16)
        g_ref[...] = _sigmoid(_nt_dot(wg_ref[...], hv))

    return pl.pallas_call(
        body,
        grid=(t // tq,),
        in_specs=[pl.BlockSpec((tq, d), lambda i: (i, 0)),
                  pl.BlockSpec((2 * nv, d), lambda i: (0, 0)),
                  pl.BlockSpec((ng, d), lambda i: (0, 0))],
        out_specs=[pl.BlockSpec((1, nv, tq), lambda i: (i, 0, 0)),
                   pl.BlockSpec((per, nv, WIN_CHUNK), lambda i: (i, 0, 0)),
                   pl.BlockSpec((ng, tq), lambda i: (0, i))],
        out_shape=[jax.ShapeDtypeStruct((t // tq, nv, tq), BF16),
                   jax.ShapeDtypeStruct((t // WIN_CHUNK, nv, WIN_CHUNK), BF16),
                   jax.ShapeDtypeStruct((ng, t), F32)],
        compiler_params=_cparams("parallel"),
        name="nt_proj",
    )(h, w_vt, w_gt)


def _compress_call(xc, wa, wb, pe8, w2, *, transposed, cc):
    nc, kk = xc.shape

    def body(x_ref, wa_ref, wb_ref, pe_ref, w2_ref, o_ref):
        xv = x_ref[...]
        top = jnp.dot(xv, wa_ref[...], preferred_element_type=F32)
        bot = jnp.dot(xv, wb_ref[...], preferred_element_type=F32)
        pe_top = jnp.dot(pe_ref[...], wa_ref[...], preferred_element_type=F32)
        pe_bot = jnp.dot(pe_ref[...], wb_ref[...], preferred_element_type=F32)
        pre = top + pltpu.roll(bot, nc - 1, 0) + pe_top[0:1] + pe_bot[1:2]
        hid = (pre * _sigmoid(pre)).astype(BF16)
        if transposed:
            r = _nt_dot(w2_ref[...], hid)
            for c in range(nc // cc):
                o_ref[c] = r[:, c * cc:(c + 1) * cc].astype(BF16)
        else:
            o_ref[...] = jnp.dot(hid, w2_ref[...], preferred_element_type=F32).astype(BF16)

    if transposed:
        out_spec = pl.BlockSpec((None, nc // cc, HEAD_DIM, cc), lambda g: (g, 0, 0, 0))
        out_shape = jax.ShapeDtypeStruct((N_GROUPS, nc // cc, HEAD_DIM, cc), BF16)
    else:
        out_spec = pl.BlockSpec((None, nc, HEAD_DIM), lambda g: (g, 0, 0))
        out_shape = jax.ShapeDtypeStruct((N_GROUPS, nc, HEAD_DIM), BF16)
    return pl.pallas_call(
        body,
        grid=(N_GROUPS,),
        in_specs=[pl.BlockSpec((nc, kk), lambda g: (0, 0)),
                  pl.BlockSpec((None, kk, CMP_HIDDEN), lambda g: (g, 0, 0)),
                  pl.BlockSpec((None, kk, CMP_HIDDEN), lambda g: (g, 0, 0)),
                  pl.BlockSpec((8, kk), lambda g: (0, 0)),
                  pl.BlockSpec(w2.shape, lambda g: (0, 0))],
        out_specs=out_spec,
        out_shape=out_shape,
        compiler_params=_cparams("parallel"),
        name="compress_v" if transposed else "compress_k",
    )(xc, wa, wb, pe8, w2)


def _nsa_call(proj, kc, vct, vst, vwt, gates, slopes, *, t, ks_col, kw_col):
    nq = t // Q_TILE
    n_sel = t // SEL_BLOCK
    nc = t // CMP_STRIDE
    cc = min(CMP_CHUNK, nc)
    top_n = min(SEL_TOP_N, n_sel)
    hg = HEADS_PER_GROUP
    wide = hg * Q_TILE
    sel_per_chunk = SLC_CHUNK // SEL_BLOCK

    def body(q_ref, kc_ref, vct_ref, ks_ref, vst_ref, kw_ref, vwt_ref, g_ref, sl_ref, o_ref,
             qc_ref, m_ref, l_ref, acc_ref, out_ref, s_ref, p_ref, ps_ref, score_ref, sel_ref):
        g = pl.program_id(0)
        i = pl.program_id(1)
        t0 = i * Q_TILE
        tq = t0 + lax.broadcasted_iota(jnp.int32, (1, Q_TILE), 1)

        for h in range(hg):
            qc_ref[h * Q_TILE:(h + 1) * Q_TILE, :] = q_ref[:, h * HEAD_DIM:(h + 1) * HEAD_DIM]

        def reset():
            m_ref[...] = jnp.full((1, wide), NEG_INF, F32)
            l_ref[...] = jnp.zeros((1, wide), F32)
            acc_ref[...] = jnp.zeros((HEAD_DIM, wide), F32)

        def attend(k_c, vt_c, mask, dpos, nrows, s_store=None):
            st = _nt_dot(k_c, qc_ref[...])
            for h in range(hg):
                sl = slice(h * Q_TILE, (h + 1) * Q_TILE)
                s = jnp.where(mask, st[:, sl] + sl_ref[:, sl] * dpos, NEG_INF)
                if s_store is not None:
                    s_store(sl, s)
                m_old = m_ref[:, sl]
                m_new = jnp.maximum(m_old, jnp.max(s, axis=0, keepdims=True))
                alpha = jnp.exp(m_old - m_new)
                p = jnp.exp(s - m_new)
                l_ref[:, sl] = alpha * l_ref[:, sl] + jnp.sum(p, axis=0, keepdims=True)
                m_ref[:, sl] = m_new
                acc_ref[:, sl] = acc_ref[:, sl] * alpha
                p_ref[0:nrows, sl] = p.astype(BF16)
            acc_ref[...] += jnp.dot(vt_c, p_ref[0:nrows, :], preferred_element_type=F32)

        def inv_norm(sl):
            valid = m_ref[:, sl] > 0.5 * NEG_INF
            return jnp.where(valid, 1.0 / jnp.maximum(l_ref[:, sl], 1e-30), 0.0)

        def finish(branch, first):
            for h in range(hg):
                sl = slice(h * Q_TILE, (h + 1) * Q_TILE)
                gate = g_ref[pl.ds(branch * N_HEADS + g * hg + h, 1), :]
                contrib = acc_ref[:, sl] * (inv_norm(sl) * gate)
                if first:
                    out_ref[:, sl] = contrib
                else:
                    out_ref[:, sl] += contrib

        reset()
        ps_ref[...] = jnp.zeros(ps_ref.shape, F32)
        n_cmp_keys = (t0 + Q_TILE - CMP_BLOCK) // CMP_STRIDE + 1
        n_cc = (n_cmp_keys + cc - 1) // cc

        def cmp_body(c, carry):
            r0 = pl.multiple_of(c * cc, cc)
            n = r0 + lax.broadcasted_iota(jnp.int32, (cc, Q_TILE), 0)
            dposi = (n * CMP_STRIDE + (CMP_BLOCK - 1)) - tq

            def store(sl, s):
                s_ref[pl.ds(r0, cc), sl] = s

            attend(kc_ref[pl.ds(r0, cc), :], vct_ref[c], dposi <= 0, dposi.astype(F32), cc, store)
            return carry

        lax.fori_loop(0, n_cc, cmp_body, 0)
        finish(0, True)

        def ps_body(c, carry):
            r0 = pl.multiple_of(c * cc, cc)
            tot = jnp.zeros((cc, Q_TILE), F32)
            for h in range(hg):
                sl = slice(h * Q_TILE, (h + 1) * Q_TILE)
                tot = tot + jnp.exp(s_ref[pl.ds(r0, cc), sl] - m_ref[:, sl]) * inv_norm(sl)
            ps_ref[pl.ds(8 + r0, cc), :] = tot
            return carry

        lax.fori_loop(0, n_cc, ps_body, 0)

        ratio = SEL_BLOCK // CMP_STRIDE
        imp = ps_ref[pl.ds(7, n_sel, stride=ratio), :]
        for kk in range(ratio):
            imp = imp + ps_ref[pl.ds(8 + kk, n_sel, stride=ratio), :]
        jrow = lax.broadcasted_iota(jnp.int32, (n_sel, Q_TILE), 0)
        cur = tq // SEL_BLOCK
        bonus = jnp.where(jrow == 0, FORCE_BONUS,
                          jnp.where(jrow == cur, FORCE_BONUS,
                                    jnp.where(jrow == cur - 1, FORCE_BONUS, 0.0)))
        score_ref[...] = jnp.where(jrow <= cur, imp + bonus, -1.0)
        sel_ref[...] = jnp.zeros((n_sel, Q_TILE), F32)
        jrow_f = jrow.astype(F32)

        def topk_body(r, carry):
            sc = score_ref[...]
            mx = jnp.max(sc, axis=0, keepdims=True)
            first = jnp.min(jnp.where(sc == mx, jrow_f, float(n_sel)), axis=0, keepdims=True)
            pick = jrow_f == first
            sel_ref[...] = jnp.where(pick, 1.0, sel_ref[...])
            score_ref[...] = jnp.where(pick, -2.0, sc)
            return carry

        lax.fori_loop(0, top_n, topk_body, 0)

        reset()
        n_sc = (t0 + Q_TILE - 1) // SLC_CHUNK + 1

        def slc_body(c, carry):
            r0 = pl.multiple_of(c * SLC_CHUNK, SLC_CHUNK)
            sel8 = sel_ref[pl.ds(pl.multiple_of(c * sel_per_chunk, sel_per_chunk), sel_per_chunk), :]
            selb = jnp.concatenate(
                [jnp.broadcast_to(sel8[b:b + 1, :], (SEL_BLOCK, Q_TILE)) for b in range(sel_per_chunk)], axis=0)
            dposi = (r0 + lax.broadcasted_iota(jnp.int32, (SLC_CHUNK, Q_TILE), 0)) - tq
            mask = jnp.where(dposi <= 0, selb, 0.0) > 0.5
            attend(ks_ref[pl.ds(r0, SLC_CHUNK), :], vst_ref[c], mask, dposi.astype(F32), SLC_CHUNK)
            return carry

        lax.fori_loop(0, n_sc, slc_body, 0)
        finish(1, False)

        reset()
        n_wc = WINDOW // WIN_CHUNK + Q_TILE // WIN_CHUNK
        first_chunk = i * (Q_TILE // WIN_CHUNK) - WINDOW // WIN_CHUNK

        def win_body(kk, carry):
            cw = first_chunk + kk
            r0 = pl.multiple_of(cw * WIN_CHUNK, WIN_CHUNK)
            dposi = (r0 + lax.broadcasted_iota(jnp.int32, (WIN_CHUNK, Q_TILE), 0)) - tq
            mask = jnp.where(dposi <= 0, dposi, -2 * WINDOW) > -WINDOW
            attend(kw_ref[pl.ds(r0, WIN_CHUNK), :], vwt_ref[cw], mask, dposi.astype(F32), WIN_CHUNK)
            return carry

        lax.fori_loop(jnp.maximum(0, -first_chunk), n_wc, win_body, 0)
        finish(2, False)

        for h in range(hg):
            sl = slice(h * Q_TILE, (h + 1) * Q_TILE)
            o_ref[:, h * HEAD_DIM:(h + 1) * HEAD_DIM] = out_ref[:, sl].T.astype(o_ref.dtype)

    one = pl.Buffered(1)
    in_specs = [
        pl.BlockSpec((Q_TILE, hg * HEAD_DIM), lambda g, i: (i, g)),
        pl.BlockSpec((None, nc, HEAD_DIM), lambda g, i: (g, 0, 0)),
        pl.BlockSpec((None, nc // cc, HEAD_DIM, cc), lambda g, i: (g, 0, 0, 0)),
        pl.BlockSpec((t, HEAD_DIM), lambda g, i: (0, ks_col + g), pipeline_mode=one),
        pl.BlockSpec((t // SLC_CHUNK, HEAD_DIM, SLC_CHUNK), lambda g, i: (0, g, 0), pipeline_mode=one),
        pl.BlockSpec((t, HEAD_DIM), lambda g, i: (0, kw_col + g), pipeline_mode=one),
        pl.BlockSpec((t // WIN_CHUNK, HEAD_DIM, WIN_CHUNK), lambda g, i: (0, g, 0), pipeline_mode=one),
        pl.BlockSpec((gates.shape[0], Q_TILE), lambda g, i: (0, i)),
        pl.BlockSpec((None, 1, wide), lambda g, i: (g, 0, 0)),
    ]
    scratch = [
        pltpu.VMEM((wide, HEAD_DIM), BF16),
        pltpu.VMEM((1, wide), F32),
        pltpu.VMEM((1, wide), F32),
        pltpu.VMEM((HEAD_DIM, wide), F32),
        pltpu.VMEM((HEAD_DIM, wide), F32),
        pltpu.VMEM((nc, wide), F32),
        pltpu.VMEM((SLC_CHUNK, wide), BF16),
        pltpu.VMEM((8 + nc, Q_TILE), F32),
        pltpu.VMEM((n_sel, Q_TILE), F32),
        pltpu.VMEM((n_sel, Q_TILE), F32),
    ]
    return pl.pallas_call(
        body,
        grid=(N_GROUPS, nq),
        in_specs=in_specs,
        out_specs=pl.BlockSpec((Q_TILE, hg * HEAD_DIM), lambda g, i: (i, g)),
        out_shape=jax.ShapeDtypeStruct((t, N_HEADS * HEAD_DIM), BF16),
        scratch_shapes=scratch,
        compiler_params=_cparams("parallel", "parallel"),
        name="nsa",
    )(proj, kc, vct, proj, vst, proj, vwt, gates, slopes)


def _conv_front_call(gm, w_dw, b_dw, ln_g, ln_b):
    t = gm.shape[0]
    ch = w_dw.shape[1]
    tm = 64
    halo = 32
    rb = 32
    strip = 512
    pad = halo - (CONV_WIDTH - 1)
    w_pad = jnp.concatenate([w_dw, jnp.zeros((halo - CONV_WIDTH, ch), F32)], axis=0)

    def body(a_ref, g_ref, ah_ref, gh_ref, w_ref, b_ref, lg_ref, lb_ref, o_ref, u_ref, y_ref):
        i = pl.program_id(0)
        av = a_ref[...].astype(F32)
        gv = g_ref[...].astype(F32)
        u_ref[halo:halo + tm, :] = av * _sigmoid(gv)
        ahv = ah_ref[...].astype(F32)
        ghv = gh_ref[...].astype(F32)
        u_ref[0:halo, :] = jnp.where(i > 0, ahv * _sigmoid(ghv), 0.0)
        for r in range(tm // rb):
            for cs in range(ch // strip):
                cols = slice(cs * strip, (cs + 1) * strip)
                acc = jnp.broadcast_to(b_ref[:, cols], (rb, strip))
                for k in range(CONV_WIDTH):
                    acc = acc + u_ref[r * rb + pad + k:r * rb + pad + k + rb, cols] * w_ref[k:k + 1, cols]
                y_ref[r * rb:(r + 1) * rb, cols] = acc
        y = y_ref[...]
        mu = jnp.mean(y, axis=-1, keepdims=True)
        yc = y - mu
        var = jnp.mean(yc * yc, axis=-1, keepdims=True)
        z = yc * lax.rsqrt(var + EPS) * lg_ref[...] + lb_ref[...]
        o_ref[...] = (z * _sigmoid(z)).astype(o_ref.dtype)

    per = tm // halo
    return pl.pallas_call(
        body,
        grid=(t // tm,),
        in_specs=[pl.BlockSpec((tm, ch), lambda i: (i, 0)),
                  pl.BlockSpec((tm, ch), lambda i: (i, 1)),
                  pl.BlockSpec((halo, ch), lambda i: (jnp.maximum(i * per - 1, 0), 0)),
                  pl.BlockSpec((halo, ch), lambda i: (jnp.maximum(i * per - 1, 0), 1)),
                  pl.BlockSpec((halo, ch), lambda i: (0, 0)),
                  pl.BlockSpec((1, ch), lambda i: (0, 0)),
                  pl.BlockSpec((1, ch), lambda i: (0, 0)),
                  pl.BlockSpec((1, ch), lambda i: (0, 0))],
        out_specs=pl.BlockSpec((tm, ch), lambda i: (i, 0)),
        out_shape=jax.ShapeDtypeStruct((t, ch), BF16),
        scratch_shapes=[pltpu.VMEM((halo + tm, ch), F32), pltpu.VMEM((tm, ch), F32)],
        compiler_params=_cparams("parallel"),
        name="conv_front",
    )(gm, gm, gm, gm, w_pad, b_dw.reshape(1, ch), ln_g.reshape(1, ch), ln_b.reshape(1, ch))


def _ffn_mid_call(up, w_dw, b_dw):
    t, n2 = up.shape
    f = n2 // 2
    tm = min(256, t)
    tc = 512
    nj = f // tc
    halo = 16
    w_pad = jnp.concatenate([w_dw, jnp.zeros((8 - FFN_CONV_WIDTH, n2), F32)], axis=0)
    b2 = b_dw.reshape(1, n2)

    def body(a_ref, g_ref, ah_ref, gh_ref, wa_ref, wg_ref, ba_ref, bg_ref, o_ref, ea_ref, eg_ref):
        i = pl.program_id(0)

        def conv(cur_ref, halo_ref, e_ref, w_ref, b_ref):
            e_ref[halo:halo + tm, :] = cur_ref[...].astype(F32)
            e_ref[0:halo, :] = jnp.where(i > 0, halo_ref[...].astype(F32), 0.0)
            acc = jnp.broadcast_to(b_ref[...], (tm, tc))
            for k in range(FFN_CONV_WIDTH):
                off = halo - (FFN_CONV_WIDTH - 1) + k
                acc = acc + e_ref[off:off + tm, :] * w_ref[k:k + 1, :]
            return acc

        ca = conv(a_ref, ah_ref, ea_ref, wa_ref, ba_ref)
        cg = conv(g_ref, gh_ref, eg_ref, wg_ref, bg_ref)
        o_ref[...] = (cg * _sigmoid(cg) * ca).astype(o_ref.dtype)

    per = tm // halo
    return pl.pallas_call(
        body,
        grid=(t // tm, nj),
        in_specs=[pl.BlockSpec((tm, tc), lambda i, j: (i, j)),
                  pl.BlockSpec((tm, tc), lambda i, j: (i, nj + j)),
                  pl.BlockSpec((halo, tc), lambda i, j: (jnp.maximum(i * per - 1, 0), j)),
                  pl.BlockSpec((halo, tc), lambda i, j: (jnp.maximum(i * per - 1, 0), nj + j)),
                  pl.BlockSpec((8, tc), lambda i, j: (0, j)),
                  pl.BlockSpec((8, tc), lambda i, j: (0, nj + j)),
                  pl.BlockSpec((1, tc), lambda i, j: (0, j)),
                  pl.BlockSpec((1, tc), lambda i, j: (0, nj + j))],
        out_specs=pl.BlockSpec((tm, tc), lambda i, j: (i, j)),
        out_shape=jax.ShapeDtypeStruct((t, f), BF16),
        scratch_shapes=[pltpu.VMEM((halo + tm, tc), F32), pltpu.VMEM((halo + tm, tc), F32)],
        compiler_params=_cparams("parallel", "parallel"),
        name="ffn_mid",
    )(up, up, up, up, w_pad, w_pad, b2, b2)


def _layer(x, c_row, p):
    t, d = x.shape
    ada = _ada_call(c_row, p["w_ada"], p["b_ada"])

    w_in = p["w_in"]
    qw = N_HEADS * HEAD_DIM
    kvw = N_GROUPS * HEAD_DIM
    o = [0, qw]
    for _ in range(6):
        o.append(o[-1] + kvw)
    o.append(o[-1] + 3 * N_HEADS)
    o.append(o[-1] + 2 * d)
    o.append(o[-1] + 2 * d)
    w_q, w_kc, w_vc, w_ks, w_vs, w_kw, w_vw, w_gn, w_glu, w_mg = [w_in[:, o[k]:o[k + 1]] for k in range(10)]
    w_rm = jnp.concatenate([w_q, w_kc, w_vc, w_ks, w_kw], axis=1).astype(BF16)
    col_scale = jnp.concatenate([jnp.full((1, qw), HEAD_DIM ** -0.5, F32),
                                 jnp.ones((1, 4 * kvw), F32)], axis=1)
    w_gm = jnp.concatenate([w_glu, w_mg], axis=1).astype(BF16)
    w_vt = jnp.concatenate([w_vs, w_vw], axis=1).T.astype(BF16)
    n_gate_rows = 64
    w_gt = jnp.concatenate([w_gn.T, jnp.zeros((n_gate_rows - 3 * N_HEADS, d), F32)], axis=0).astype(BF16)

    h1 = _norm_call(x, p["norm1_g"], ada, shift_idx=0, scale_idx=1)
    proj = _matmul_call(h1, w_rm, tm=1024, tn=1024, out_dtype=BF16, name="proj_rm",
                        epilogue=lambda acc, cs_ref: acc * cs_ref[...],
                        extras=(col_scale,), extra_specs=(pl.BlockSpec((1, 1024), lambda i, j: (0, j)),))
    gm = _matmul_call(h1, w_gm, tm=1024, tn=1024, out_dtype=BF16, name="proj_gm")
    vst, vwt, gates = _nt_proj_call(h1, w_vt, w_gt)

    nc = t // CMP_STRIDE
    cc = min(CMP_CHUNK, nc)
    half = CMP_BLOCK // CMP_STRIDE
    assert half == 2

    def expand(w1):
        w1r = w1.reshape(half, CMP_STRIDE, HEAD_DIM, CMP_HIDDEN)
        outs = []
        for hf in range(half):
            per_g = []
            for g in range(N_GROUPS):
                z = jnp.zeros((CMP_STRIDE, N_GROUPS, HEAD_DIM, CMP_HIDDEN), F32).at[:, g].set(w1r[hf])
                per_g.append(z.reshape(CMP_STRIDE * kvw, CMP_HIDDEN))
            outs.append(jnp.stack(per_g).astype(BF16))
        return outs

    pe = p["cmp_pe"].reshape(half, CMP_STRIDE, 1, HEAD_DIM)
    pe_rows = jnp.broadcast_to(pe, (half, CMP_STRIDE, N_GROUPS, HEAD_DIM)).reshape(half, CMP_STRIDE * kvw)
    pe8 = jnp.concatenate([pe_rows, jnp.zeros((8 - half, CMP_STRIDE * kvw), F32)], axis=0).astype(BF16)
    kc_raw = proj[:, qw:qw + kvw].reshape(nc, CMP_STRIDE * kvw)
    vc_raw = proj[:, qw + kvw:qw + 2 * kvw].reshape(nc, CMP_STRIDE * kvw)
    wka, wkb = expand(p["w_kc1"])
    wva, wvb = expand(p["w_vc1"])
    kc = _compress_call(kc_raw, wka, wkb, pe8, p["w_kc2"].astype(BF16), transposed=False, cc=cc)
    vct = _compress_call(vc_raw, wva, wvb, pe8, p["w_vc2"].T.astype(BF16), transposed=True, cc=cc)

    head_ids = jnp.arange(1, N_HEADS + 1, dtype=F32)
    slopes = jnp.exp2(-8.0 * head_ids / N_HEADS).reshape(N_GROUPS, 1, HEADS_PER_GROUP, 1)
    slopes = jnp.broadcast_to(slopes, (N_GROUPS, 1, HEADS_PER_GROUP, Q_TILE)).reshape(
        N_GROUPS, 1, HEADS_PER_GROUP * Q_TILE)
    o_nsa = _nsa_call(proj, kc, vct, vst, vwt, gates, slopes, t=t,
                      ks_col=(qw + 2 * kvw) // HEAD_DIM, kw_col=(qw + 3 * kvw) // HEAD_DIM)

    u2 = _conv_front_call(gm, p["conv_dw_w"], p["conv_dw_b"], p["conv_ln_g"], p["conv_ln_b"])

    tn = 1024
    nb = d // tn
    ya = _matmul_call(o_nsa, p["w_o_nsa"].astype(BF16), tm=512, tn=tn, out_dtype=F32, name="o_proj",
                      epilogue=lambda acc, ga_ref: acc * _sigmoid(ga_ref[...].astype(F32)),
                      extras=(gm,), extra_specs=(pl.BlockSpec((512, tn), lambda i, j: (i, 2 * nb + j)),))
    merged = _matmul_call(
        u2, p["conv_pw_w"].astype(BF16), tm=512, tn=tn, out_dtype=BF16, name="pw_merge",
        epilogue=lambda acc, b_ref, gb_ref, ya_ref: (acc + b_ref[...]) * _sigmoid(gb_ref[...].astype(F32)) + ya_ref[...],
        extras=(p["conv_pw_b"].reshape(1, d), gm, ya),
        extra_specs=(pl.BlockSpec((1, tn), lambda i, j: (0, j)),
                     pl.BlockSpec((512, tn), lambda i, j: (i, 3 * nb + j)),
                     pl.BlockSpec((512, tn), lambda i, j: (i, j))))
    x1 = _matmul_call(
        merged, p["w_out"].astype(BF16), tm=512, tn=tn, out_dtype=F32, name="out_proj",
        epilogue=lambda acc, x_ref, g_ref: x_ref[...] + g_ref[0:1, :] * acc,
        extras=(x, ada),
        extra_specs=(pl.BlockSpec((512, tn), lambda i, j: (i, j)),
                     pl.BlockSpec((8, tn), lambda i, j: (0, 2 * nb + j))))

    h2 = _norm_call(x1, p["norm2_g"], ada, shift_idx=3, scale_idx=4)
    up = _matmul_call(h2, p["ffn_w_up"].astype(BF16), tm=1024, tn=1024, out_dtype=BF16, name="ffn_up")
    act = _ffn_mid_call(up, p["ffn_dw_w"], p["ffn_dw_b"])
    x2 = _matmul_call(
        act, p["ffn_w_down"].astype(BF16), tm=512, tn=512, out_dtype=F32, name="ffn_down",
        epilogue=lambda acc, x_ref, g_ref: x_ref[...] + g_ref[0:1, :] * acc,
        extras=(x1, ada),
        extra_specs=(pl.BlockSpec((512, 512), lambda i, j: (i, j)),
                     pl.BlockSpec((8, 512), lambda i, j: (0, 5 * (d // 512) + j))))
    return x2


def kernel(x, c, w_ada, b_ada, norm1_g, w_in, cmp_pe, w_kc1, w_kc2, w_vc1, w_vc2, w_o_nsa, conv_dw_w, conv_dw_b, conv_ln_g, conv_ln_b, conv_pw_w, conv_pw_b, w_out, norm2_g, ffn_w_up, ffn_dw_w, ffn_dw_b, ffn_w_down, final_g):
    b, t, d = x.shape
    stacked = dict(w_ada=w_ada, b_ada=b_ada, norm1_g=norm1_g, w_in=w_in, cmp_pe=cmp_pe, w_kc1=w_kc1,
                   w_kc2=w_kc2, w_vc1=w_vc1, w_vc2=w_vc2, w_o_nsa=w_o_nsa, conv_dw_w=conv_dw_w,
                   conv_dw_b=conv_dw_b, conv_ln_g=conv_ln_g, conv_ln_b=conv_ln_b, conv_pw_w=conv_pw_w,
                   conv_pw_b=conv_pw_b, w_out=w_out, norm2_g=norm2_g, ffn_w_up=ffn_w_up,
                   ffn_dw_w=ffn_dw_w, ffn_dw_b=ffn_dw_b, ffn_w_down=ffn_w_down)
    depth = w_ada.shape[0]
    outs = []
    for bi in range(b):
        xb = x[bi]
        for layer in range(depth):
            xb = _layer(xb, c[bi:bi + 1], {k: v[layer] for k, v in stacked.items()})
        outs.append(_norm_call(xb, final_g, out_dtype=x.dtype))
    return jnp.stack(outs)
```

```python
import functools

import jax
import jax.numpy as jnp
from jax import lax
from jax.experimental import pallas as pl
from jax.experimental.pallas import tpu as pltpu

F32 = jnp.float32
BF16 = jnp.bfloat16

N_HEADS = 16
N_GROUPS = 2
HEADS_PER_GROUP = N_HEADS // N_GROUPS
HEAD_DIM = 128
CMP_BLOCK = 32
CMP_STRIDE = 16
CMP_HIDDEN = 2 * HEAD_DIM
SEL_BLOCK = 64
SEL_TOP_N = 16
WINDOW = 512
FORCE_BONUS = 1e6
CONV_WIDTH = 31
FFN_CONV_WIDTH = 3
EPS = 1e-6
NEG_INF = -1e30
LOG2E = 1.4426950408889634

V7X_VMEM_BYTES = 64 * 1024 * 1024
VMEM_LIMIT_BYTES = V7X_VMEM_BYTES - 8 * 1024 * 1024
LANES = 128
F32_SUBLANES = 8

Q_TILE = 128
SLC_CHUNK = 512
WIN_CHUNK = 128
CMP_CHUNK = 256


def _cparams(*sem):
    return pltpu.CompilerParams(dimension_semantics=sem, vmem_limit_bytes=VMEM_LIMIT_BYTES)


def _sigmoid(v):
    return 1.0 / (1.0 + jnp.exp(-v))


def _nt_dot(a, b):
    return lax.dot_general(a, b, (((1,), (1,)), ((), ())), preferred_element_type=F32)


def _ada_call(c_row, w, b):
    d, n = w.shape
    tn = 1024
    cb = jnp.broadcast_to(c_row.reshape(d, 1), (d, LANES))

    def body(cb_ref, w_ref, b_ref, o_ref):
        cv = cb_ref[...]
        act = cv * _sigmoid(cv)
        for j in range(tn // LANES):
            sl = slice(j * LANES, (j + 1) * LANES)
            row = jnp.sum(w_ref[:, sl] * act, axis=0, keepdims=True) + b_ref[:, sl]
            o_ref[:, sl] = jnp.broadcast_to(row, (8, LANES))

    return pl.pallas_call(
        body,
        grid=(n // tn,),
        in_specs=[pl.BlockSpec((d, LANES), lambda j: (0, 0)),
                  pl.BlockSpec((d, tn), lambda j: (0, j)),
                  pl.BlockSpec((1, tn), lambda j: (0, j))],
        out_specs=pl.BlockSpec((8, tn), lambda j: (0, j)),
        out_shape=jax.ShapeDtypeStruct((8, n), F32),
        compiler_params=_cparams("parallel"),
        name="ada",
    )(cb, w, b.reshape(1, n))


def _norm_call(x, gain, ada=None, shift_idx=0, scale_idx=0, out_dtype=BF16):
    t, d = x.shape
    tm = 512

    def body(*refs):
        if ada is None:
            x_ref, g_ref, o_ref = refs
        else:
            x_ref, g_ref, sh_ref, sc_ref, o_ref = refs
        xv = x_ref[...]
        y = xv * lax.rsqrt(jnp.mean(xv * xv, axis=-1, keepdims=True) + EPS) * g_ref[...]
        if ada is not None:
            y = y * (1.0 + sc_ref[0:1, :]) + sh_ref[0:1, :]
        o_ref[...] = y.astype(o_ref.dtype)

    in_specs = [pl.BlockSpec((tm, d), lambda i: (i, 0)), pl.BlockSpec((1, d), lambda i: (0, 0))]
    args = [x, gain.reshape(1, d)]
    if ada is not None:
        in_specs += [pl.BlockSpec((8, d), lambda i: (0, shift_idx)),
                     pl.BlockSpec((8, d), lambda i: (0, scale_idx))]
        args += [ada, ada]
    return pl.pallas_call(
        body,
        grid=(t // tm,),
        in_specs=in_specs,
        out_specs=pl.BlockSpec((tm, d), lambda i: (i, 0)),
        out_shape=jax.ShapeDtypeStruct((t, d), out_dtype),
        compiler_params=_cparams("parallel"),
        name="norm",
    )(*args)


def _matmul_call(a, w, *, tm, tn, out_dtype, name, epilogue=None, extras=(), extra_specs=()):
    m, k = a.shape
    n = w.shape[1]
    tm = min(tm, m)

    def body(a_ref, w_ref, *rest):
        o_ref = rest[-1]
        acc = jnp.dot(a_ref[...], w_ref[...], preferred_element_type=F32)
        if epilogue is not None:
            acc = epilogue(acc, *rest[:-1])
        o_ref[...] = acc.astype(o_ref.dtype)

    return pl.pallas_call(
        body,
        grid=(m // tm, n // tn),
        in_specs=[pl.BlockSpec((tm, k), lambda i, j: (i, 0)),
                  pl.BlockSpec((k, tn), lambda i, j: (0, j))] + list(extra_specs),
        out_specs=pl.BlockSpec((tm, tn), lambda i, j: (i, j)),
        out_shape=jax.ShapeDtypeStruct((m, n), out_dtype),
        compiler_params=_cparams("parallel", "parallel"),
        name=name,
    )(a, w, *extras)


def _nt_proj_call(h, w_vt, w_gt):
    t, d = h.shape
    tq = SLC_CHUNK
    nv = w_vt.shape[0] // 2
    ng = w_gt.shape[0]
    per = tq // WIN_CHUNK

    def body(h_ref, wv_ref, wg_ref, vs_ref, vw_ref, g_ref):
        hv = h_ref[...]
        r = _nt_dot(wv_ref[...], hv)
        vs_ref[0] = r[0:nv].astype(BF16)
        for kk in range(per):
            vw_ref[kk] = r[nv:2 * nv, kk * WIN_CHUNK:(kk + 1) * WIN_CHUNK].astype(BF16)
        g_ref[...] = _sigmoid(_nt_dot(wg_ref[...], hv))

    return pl.pallas_call(
        body,
        grid=(t // tq,),
        in_specs=[pl.BlockSpec((tq, d), lambda i: (i, 0)),
                  pl.BlockSpec((2 * nv, d), lambda i: (0, 0)),
                  pl.BlockSpec((ng, d), lambda i: (0, 0))],
        out_specs=[pl.BlockSpec((1, nv, tq), lambda i: (i, 0, 0)),
                   pl.BlockSpec((per, nv, WIN_CHUNK), lambda i: (i, 0, 0)),
                   pl.BlockSpec((ng, tq), lambda i: (0, i))],
        out_shape=[jax.ShapeDtypeStruct((t // tq, nv, tq), BF16),
                   jax.ShapeDtypeStruct((t // WIN_CHUNK, nv, WIN_CHUNK), BF16),
                   jax.ShapeDtypeStruct((ng, t), F32)],
        compiler_params=_cparams("parallel"),
        name="nt_proj",
    )(h, w_vt, w_gt)


def _compress_call(xc, wa, wb, pe8, w2, *, transposed, cc):
    nc, kk = xc.shape

    def body(x_ref, wa_ref, wb_ref, pe_ref, w2_ref, o_ref):
        xv = x_ref[...]
        top = jnp.dot(xv, wa_ref[...], preferred_element_type=F32)
        bot = jnp.dot(xv, wb_ref[...], preferred_element_type=F32)
        pe_top = jnp.dot(pe_ref[...], wa_ref[...], preferred_element_type=F32)
        pe_bot = jnp.dot(pe_ref[...], wb_ref[...], preferred_element_type=F32)
        pre = top + pltpu.roll(bot, nc - 1, 0) + pe_top[0:1] + pe_bot[1:2]
        hid = (pre * _sigmoid(pre)).astype(BF16)
        if transposed:
            r = _nt_dot(w2_ref[...], hid)
            for c in range(nc // cc):
                o_ref[c] = r[:, c * cc:(c + 1) * cc].astype(BF16)
        else:
            o_ref[...] = jnp.dot(hid, w2_ref[...], preferred_element_type=F32).astype(BF16)

    if transposed:
        out_spec = pl.BlockSpec((None, nc // cc, HEAD_DIM, cc), lambda g: (g, 0, 0, 0))
        out_shape = jax.ShapeDtypeStruct((N_GROUPS, nc // cc, HEAD_DIM, cc), BF16)
    else:
        out_spec = pl.BlockSpec((None, nc, HEAD_DIM), lambda g: (g, 0, 0))
        out_shape = jax.ShapeDtypeStruct((N_GROUPS, nc, HEAD_DIM), BF16)
    return pl.pallas_call(
        body,
        grid=(N_GROUPS,),
        in_specs=[pl.BlockSpec((nc, kk), lambda g: (0, 0)),
                  pl.BlockSpec((None, kk, CMP_HIDDEN), lambda g: (g, 0, 0)),
                  pl.BlockSpec((None, kk, CMP_HIDDEN), lambda g: (g, 0, 0)),
                  pl.BlockSpec((8, kk), lambda g: (0, 0)),
                  pl.BlockSpec(w2.shape, lambda g: (0, 0))],
        out_specs=out_spec,
        out_shape=out_shape,
        compiler_params=_cparams("parallel"),
        name="compress_v" if transposed else "compress_k",
    )(xc, wa, wb, pe8, w2)


def _nsa_call(proj, kc, vct, vst, vwt, gates, slopes, *, t, ks_col, kw_col):
    nq = t // Q_TILE
    n_sel = t // SEL_BLOCK
    nc = t // CMP_STRIDE
    cc = min(CMP_CHUNK, nc)
    top_n = min(SEL_TOP_N, n_sel)
    hg = HEADS_PER_GROUP
    wide = hg * Q_TILE
    sel_per_chunk = SLC_CHUNK // SEL_BLOCK
    win_span = WINDOW + Q_TILE

    def body(q_ref, kc_ref, vct_ref, ks_ref, vst_ref, kw_ref, vwt_ref, g_ref, sl_ref, o_ref,
             qc_ref, m_ref, l_ref, acc_ref, out_ref, s_ref, p_ref, ps_ref, score_ref, sel_ref, flag_ref):
        g = pl.program_id(0)
        i = pl.program_id(1)
        t0 = i * Q_TILE
        tq = t0 + lax.broadcasted_iota(jnp.int32, (1, Q_TILE), 1)

        for h in range(hg):
            qc_ref[h * Q_TILE:(h + 1) * Q_TILE, :] = q_ref[:, h * HEAD_DIM:(h + 1) * HEAD_DIM]

        def reset():
            m_ref[...] = jnp.full((1, wide), NEG_INF, F32)
            l_ref[...] = jnp.zeros((1, wide), F32)
            acc_ref[...] = jnp.zeros((HEAD_DIM, wide), F32)

        def attend(k_c, vt_c, mask, dpos, nrows, s_store=None):
            st = _nt_dot(k_c, qc_ref[...])
            for h in range(hg):
                sl = slice(h * Q_TILE, (h + 1) * Q_TILE)
                s = jnp.where(mask, st[:, sl] + sl_ref[:, sl] * dpos, NEG_INF)
                if s_store is not None:
                    s_store(sl, s)
                m_old = m_ref[:, sl]
                m_new = jnp.maximum(m_old, jnp.max(s, axis=0, keepdims=True))
                alpha = jnp.exp2(m_old - m_new)
                p = jnp.exp2(s - m_new)
                l_ref[:, sl] = alpha * l_ref[:, sl] + jnp.sum(p, axis=0, keepdims=True)
                m_ref[:, sl] = m_new
                acc_ref[:, sl] = acc_ref[:, sl] * alpha
                p_ref[0:nrows, sl] = p.astype(BF16)
            acc_ref[...] += jnp.dot(vt_c, p_ref[0:nrows, :], preferred_element_type=F32)

        def inv_norm(sl):
            valid = m_ref[:, sl] > 0.5 * NEG_INF
            return jnp.where(valid, 1.0 / jnp.maximum(l_ref[:, sl], 1e-30), 0.0)

        def finish(branch, first):
            for h in range(hg):
                sl = slice(h * Q_TILE, (h + 1) * Q_TILE)
                gate = g_ref[pl.ds(branch * N_HEADS + g * hg + h, 1), :]
                contrib = acc_ref[:, sl] * (inv_norm(sl) * gate)
                if first:
                    out_ref[:, sl] = contrib
                else:
                    out_ref[:, sl] += contrib

        reset()
        ps_ref[...] = jnp.zeros(ps_ref.shape, F32)
        n_cmp_keys = (t0 + Q_TILE - CMP_BLOCK) // CMP_STRIDE + 1
        n_cc = (n_cmp_keys + cc - 1) // cc

        def cmp_body(c, carry):
            r0 = pl.multiple_of(c * cc, cc)
            n = r0 + lax.broadcasted_iota(jnp.int32, (cc, Q_TILE), 0)
            dposi = (n * CMP_STRIDE + (CMP_BLOCK - 1)) - tq

            def store(sl, s):
                s_ref[pl.ds(r0, cc), sl] = s

            attend(kc_ref[pl.ds(r0, cc), :], vct_ref[c], dposi <= 0, dposi.astype(F32), cc, store)
            return carry

        lax.fori_loop(0, n_cc, cmp_body, 0)
        finish(0, True)

        def ps_body(c, carry):
            r0 = pl.multiple_of(c * cc, cc)
            tot = jnp.zeros((cc, Q_TILE), F32)
            for h in range(hg):
                sl = slice(h * Q_TILE, (h + 1) * Q_TILE)
                tot = tot + jnp.exp2(s_ref[pl.ds(r0, cc), sl] - m_ref[:, sl]) * inv_norm(sl)
            ps_ref[pl.ds(8 + r0, cc), :] = tot
            return carry

        lax.fori_loop(0, n_cc, ps_body, 0)

        ratio = SEL_BLOCK // CMP_STRIDE
        imp = ps_ref[pl.ds(7, n_sel, stride=ratio), :]
        for kk in range(ratio):
            imp = imp + ps_ref[pl.ds(8 + kk, n_sel, stride=ratio), :]
        jrow = lax.broadcasted_iota(jnp.int32, (n_sel, Q_TILE), 0)
        cur = tq // SEL_BLOCK
        bonus = jnp.where(jrow == 0, FORCE_BONUS,
                          jnp.where(jrow == cur, FORCE_BONUS,
                                    jnp.where(jrow == cur - 1, FORCE_BONUS, 0.0)))
        score_ref[...] = jnp.where(jrow <= cur, imp + bonus, -1.0)
        sel_ref[...] = jnp.zeros((n_sel, Q_TILE), F32)
        jrow_f = jrow.astype(F32)

        def topk_body(r, carry):
            sc = score_ref[...]
            mx = jnp.max(sc, axis=0, keepdims=True)
            first = jnp.min(jnp.where(sc == mx, jrow_f, float(n_sel)), axis=0, keepdims=True)
            pick = jrow_f == first
            sel_ref[...] = jnp.where(pick, 1.0, sel_ref[...])
            score_ref[...] = jnp.where(pick, -2.0, sc)
            return carry

        lax.fori_loop(0, top_n, topk_body, 0)
        for c in range(t // SLC_CHUNK):
            blocks = sel_ref[c * sel_per_chunk:(c + 1) * sel_per_chunk, :]
            flag_ref[c] = (jnp.max(blocks) > 0.5).astype(jnp.int32)

        reset()
        n_sc = (t0 + Q_TILE - 1) // SLC_CHUNK + 1

        def slc_body(c, carry):
            @pl.when(flag_ref[c] > 0)
            def _():
                r0 = pl.multiple_of(c * SLC_CHUNK, SLC_CHUNK)
                sel8 = sel_ref[pl.ds(pl.multiple_of(c * sel_per_chunk, sel_per_chunk), sel_per_chunk), :]
                selb = jnp.concatenate(
                    [jnp.broadcast_to(sel8[b:b + 1, :], (SEL_BLOCK, Q_TILE)) for b in range(sel_per_chunk)], axis=0)
                dposi = (r0 + lax.broadcasted_iota(jnp.int32, (SLC_CHUNK, Q_TILE), 0)) - tq
                mask = jnp.where(dposi <= 0, selb, 0.0) > 0.5
                attend(ks_ref[pl.ds(r0, SLC_CHUNK), :], vst_ref[c], mask, dposi.astype(F32), SLC_CHUNK)

            return carry

        lax.fori_loop(0, n_sc, slc_body, 0)
        finish(1, False)

        reset()
        n_wc = win_span // WIN_CHUNK
        c0 = jnp.maximum(i * (Q_TILE // WIN_CHUNK) - WINDOW // WIN_CHUNK, 0)
        r0w = pl.multiple_of(c0 * WIN_CHUNK, WIN_CHUNK)
        dposw = (r0w + lax.broadcasted_iota(jnp.int32, (win_span, Q_TILE), 0)) - tq
        maskw = jnp.where(dposw <= 0, dposw, -2 * WINDOW) > -WINDOW
        vtw = jnp.concatenate([vwt_ref[c0 + kk] for kk in range(n_wc)], axis=1)
        attend(kw_ref[pl.ds(r0w, win_span), :], vtw, maskw, dposw.astype(F32), win_span)
        finish(2, False)

        for h in range(hg):
            sl = slice(h * Q_TILE, (h + 1) * Q_TILE)
            o_ref[:, h * HEAD_DIM:(h + 1) * HEAD_DIM] = out_ref[:, sl].T.astype(o_ref.dtype)

    one = pl.Buffered(1)
    in_specs = [
        pl.BlockSpec((Q_TILE, hg * HEAD_DIM), lambda g, i: (i, g)),
        pl.BlockSpec((None, nc, HEAD_DIM), lambda g, i: (g, 0, 0)),
        pl.BlockSpec((None, nc // cc, HEAD_DIM, cc), lambda g, i: (g, 0, 0, 0)),
        pl.BlockSpec((t, HEAD_DIM), lambda g, i: (0, ks_col + g), pipeline_mode=one),
        pl.BlockSpec((t // SLC_CHUNK, HEAD_DIM, SLC_CHUNK), lambda g, i: (0, g, 0), pipeline_mode=one),
        pl.BlockSpec((t, HEAD_DIM), lambda g, i: (0, kw_col + g), pipeline_mode=one),
        pl.BlockSpec((t // WIN_CHUNK, HEAD_DIM, WIN_CHUNK), lambda g, i: (0, g, 0), pipeline_mode=one),
        pl.BlockSpec((gates.shape[0], Q_TILE), lambda g, i: (0, i)),
        pl.BlockSpec((None, 1, wide), lambda g, i: (g, 0, 0)),
    ]
    scratch = [
        pltpu.VMEM((wide, HEAD_DIM), BF16),
        pltpu.VMEM((1, wide), F32),
        pltpu.VMEM((1, wide), F32),
        pltpu.VMEM((HEAD_DIM, wide), F32),
        pltpu.VMEM((HEAD_DIM, wide), F32),
        pltpu.VMEM((nc, wide), F32),
        pltpu.VMEM((max(SLC_CHUNK, win_span), wide), BF16),
        pltpu.VMEM((8 + nc, Q_TILE), F32),
        pltpu.VMEM((n_sel, Q_TILE), F32),
        pltpu.VMEM((n_sel, Q_TILE), F32),
        pltpu.SMEM((t // SLC_CHUNK,), jnp.int32),
    ]
    return pl.pallas_call(
        body,
        grid=(N_GROUPS, nq),
        in_specs=in_specs,
        out_specs=pl.BlockSpec((Q_TILE, hg * HEAD_DIM), lambda g, i: (i, g)),
        out_shape=jax.ShapeDtypeStruct((t, N_HEADS * HEAD_DIM), BF16),
        scratch_shapes=scratch,
        compiler_params=_cparams("parallel", "parallel"),
        name="nsa",
    )(proj, kc, vct, proj, vst, proj, vwt, gates, slopes)


def _conv_front_call(gm, w_dw, b_dw, ln_g, ln_b):
    t = gm.shape[0]
    ch = w_dw.shape[1]
    tm = min(256, t)
    halo = 32
    rb = 32
    strip = 512
    w_rep = jnp.repeat(w_dw, F32_SUBLANES, axis=0)

    def body(a_ref, g_ref, ah_ref, gh_ref, w_ref, b_ref, lg_ref, lb_ref, o_ref, u_ref, sb_ref, y_ref):
        i = pl.program_id(0)
        av = a_ref[...].astype(F32)
        gv = g_ref[...].astype(F32)
        u_ref[halo:halo + tm, :] = av * _sigmoid(gv)
        ahv = ah_ref[...].astype(F32)
        ghv = gh_ref[...].astype(F32)
        u_ref[0:halo, :] = jnp.where(i > 0, ahv * _sigmoid(ghv), 0.0)
        for cs in range(ch // strip):
            cols = slice(cs * strip, (cs + 1) * strip)
            ue = u_ref[:, cols]
            sb_ref[0] = ue
            for part in range(1, F32_SUBLANES):
                sb_ref[part] = pltpu.roll(ue, part, 0)

            def rows(r, carry, cols=cols):
                r0 = pl.multiple_of(r * rb, rb)
                acc = jnp.zeros((rb // F32_SUBLANES, F32_SUBLANES, strip), F32)
                for s in range(CONV_WIDTH):
                    whole, part = divmod(s, F32_SUBLANES)
                    k = CONV_WIDTH - 1 - s
                    tap = sb_ref[part, pl.ds(halo + r0 - F32_SUBLANES * whole, rb), :]
                    wk = w_ref[k * F32_SUBLANES:(k + 1) * F32_SUBLANES, cols]
                    acc = acc + tap.reshape(rb // F32_SUBLANES, F32_SUBLANES, strip) * wk
                y_ref[pl.ds(r0, rb), cols] = acc.reshape(rb, strip) + b_ref[:, cols]
                return carry

            lax.fori_loop(0, tm // rb, rows, 0)
        y = y_ref[...]
        mu = jnp.mean(y, axis=-1, keepdims=True)
        yc = y - mu
        var = jnp.mean(yc * yc, axis=-1, keepdims=True)
        z = yc * lax.rsqrt(var + EPS) * lg_ref[...] + lb_ref[...]
        o_ref[...] = (z * _sigmoid(z)).astype(o_ref.dtype)

    per = tm // halo
    return pl.pallas_call(
        body,
        grid=(t // tm,),
        in_specs=[pl.BlockSpec((tm, ch), lambda i: (i, 0)),
                  pl.BlockSpec((tm, ch), lambda i: (i, 1)),
                  pl.BlockSpec((halo, ch), lambda i: (jnp.maximum(i * per - 1, 0), 0)),
                  pl.BlockSpec((halo, ch), lambda i: (jnp.maximum(i * per - 1, 0), 1)),
                  pl.BlockSpec((CONV_WIDTH * F32_SUBLANES, ch), lambda i: (0, 0)),
                  pl.BlockSpec((1, ch), lambda i: (0, 0)),
                  pl.BlockSpec((1, ch), lambda i: (0, 0)),
                  pl.BlockSpec((1, ch), lambda i: (0, 0))],
        out_specs=pl.BlockSpec((tm, ch), lambda i: (i, 0)),
        out_shape=jax.ShapeDtypeStruct((t, ch), BF16),
        scratch_shapes=[pltpu.VMEM((halo + tm, ch), F32),
                        pltpu.VMEM((F32_SUBLANES, halo + tm, strip), F32),
                        pltpu.VMEM((tm, ch), F32)],
        compiler_params=_cparams("parallel"),
        name="conv_front",
    )(gm, gm, gm, gm, w_rep, b_dw.reshape(1, ch), ln_g.reshape(1, ch), ln_b.reshape(1, ch))


def _ffn_up_call(h2, w_up, w_dw, b_dw):
    t, d = h2.shape
    n2 = w_up.shape[1]
    f = n2 // 2
    tm = min(1024, t)
    tn = 512
    nj = f // tn
    sub = F32_SUBLANES
    w_pad = jnp.concatenate([w_dw, jnp.zeros((sub - FFN_CONV_WIDTH, n2), F32)], axis=0)
    b2 = b_dw.reshape(1, n2)

    def body(h_ref, wa_ref, wg_ref, da_ref, dg_ref, ba_ref, bg_ref, o_ref, ca_ref, cg_ref):
        i = pl.program_id(0)
        j = pl.program_id(1)
        hv = h_ref[...]

        def conv(w_ref, d_ref, b_ref, c_ref):
            up = jnp.dot(hv, w_ref[...], preferred_element_type=F32)

            @pl.when(i == 0)
            def _():
                c_ref[j] = jnp.zeros((sub, tn), F32)

            ext = jnp.concatenate([c_ref[j], up], axis=0)
            c_ref[j] = up[tm - sub:, :]
            acc = b_ref[...] + up * d_ref[FFN_CONV_WIDTH - 1:FFN_CONV_WIDTH, :]
            for s in range(1, FFN_CONV_WIDTH):
                k = FFN_CONV_WIDTH - 1 - s
                acc = acc + pltpu.roll(ext, s, 0)[sub:, :] * d_ref[k:k + 1, :]
            return acc

        ca = conv(wa_ref, da_ref, ba_ref, ca_ref)
        cg = conv(wg_ref, dg_ref, bg_ref, cg_ref)
        o_ref[...] = (cg * _sigmoid(cg) * ca).astype(o_ref.dtype)

    return pl.pallas_call(
        body,
        grid=(t // tm, nj),
        in_specs=[pl.BlockSpec((tm, d), lambda i, j: (i, 0)),
                  pl.BlockSpec((d, tn), lambda i, j: (0, j)),
                  pl.BlockSpec((d, tn), lambda i, j: (0, nj + j)),
                  pl.BlockSpec((sub, tn), lambda i, j: (0, j)),
                  pl.BlockSpec((sub, tn), lambda i, j: (0, nj + j)),
                  pl.BlockSpec((1, tn), lambda i, j: (0, j)),
                  pl.BlockSpec((1, tn), lambda i, j: (0, nj + j))],
        out_specs=pl.BlockSpec((tm, tn), lambda i, j: (i, j)),
        out_shape=jax.ShapeDtypeStruct((t, f), BF16),
        scratch_shapes=[pltpu.VMEM((nj, sub, tn), F32), pltpu.VMEM((nj, sub, tn), F32)],
        compiler_params=_cparams("arbitrary", "arbitrary"),
        name="ffn_up",
    )(h2, w_up, w_up, w_pad, w_pad, b2, b2)


def _layer(x, c_row, p):
    t, d = x.shape
    ada = _ada_call(c_row, p["w_ada"], p["b_ada"])

    w_in = p["w_in"]
    qw = N_HEADS * HEAD_DIM
    kvw = N_GROUPS * HEAD_DIM
    o = [0, qw]
    for _ in range(6):
        o.append(o[-1] + kvw)
    o.append(o[-1] + 3 * N_HEADS)
    o.append(o[-1] + 2 * d)
    o.append(o[-1] + 2 * d)
    w_q, w_kc, w_vc, w_ks, w_vs, w_kw, w_vw, w_gn, w_glu, w_mg = [w_in[:, o[k]:o[k + 1]] for k in range(10)]
    w_rm = jnp.concatenate([w_q, w_kc, w_vc, w_ks, w_kw], axis=1).astype(BF16)
    col_scale = jnp.concatenate([jnp.full((1, qw), HEAD_DIM ** -0.5 * LOG2E, F32),
                                 jnp.ones((1, 4 * kvw), F32)], axis=1)
    w_gm = jnp.concatenate([w_glu, w_mg], axis=1).astype(BF16)
    w_vt = jnp.concatenate([w_vs, w_vw], axis=1).T.astype(BF16)
    n_gate_rows = 64
    w_gt = jnp.concatenate([w_gn.T, jnp.zeros((n_gate_rows - 3 * N_HEADS, d), F32)], axis=0).astype(BF16)

    h1 = _norm_call(x, p["norm1_g"], ada, shift_idx=0, scale_idx=1)
    proj = _matmul_call(h1, w_rm, tm=1024, tn=1024, out_dtype=BF16, name="proj_rm",
                        epilogue=lambda acc, cs_ref: acc * cs_ref[...],
                        extras=(col_scale,), extra_specs=(pl.BlockSpec((1, 1024), lambda i, j: (0, j)),))
    gm = _matmul_call(h1, w_gm, tm=1024, tn=1024, out_dtype=BF16, name="proj_gm")
    vst, vwt, gates = _nt_proj_call(h1, w_vt, w_gt)

    nc = t // CMP_STRIDE
    cc = min(CMP_CHUNK, nc)
    half = CMP_BLOCK // CMP_STRIDE
    assert half == 2

    def expand(w1):
        w1r = w1.reshape(half, CMP_STRIDE, HEAD_DIM, CMP_HIDDEN)
        outs = []
        for hf in range(half):
            per_g = []
            for g in range(N_GROUPS):
                z = jnp.zeros((CMP_STRIDE, N_GROUPS, HEAD_DIM, CMP_HIDDEN), F32).at[:, g].set(w1r[hf])
                per_g.append(z.reshape(CMP_STRIDE * kvw, CMP_HIDDEN))
            outs.append(jnp.stack(per_g).astype(BF16))
        return outs

    pe = p["cmp_pe"].reshape(half, CMP_STRIDE, 1, HEAD_DIM)
    pe_rows = jnp.broadcast_to(pe, (half, CMP_STRIDE, N_GROUPS, HEAD_DIM)).reshape(half, CMP_STRIDE * kvw)
    pe8 = jnp.concatenate([pe_rows, jnp.zeros((8 - half, CMP_STRIDE * kvw), F32)], axis=0).astype(BF16)
    kc_raw = proj[:, qw:qw + kvw].reshape(nc, CMP_STRIDE * kvw)
    vc_raw = proj[:, qw + kvw:qw + 2 * kvw].reshape(nc, CMP_STRIDE * kvw)
    wka, wkb = expand(p["w_kc1"])
    wva, wvb = expand(p["w_vc1"])
    kc = _compress_call(kc_raw, wka, wkb, pe8, p["w_kc2"].astype(BF16), transposed=False, cc=cc)
    vct = _compress_call(vc_raw, wva, wvb, pe8, p["w_vc2"].T.astype(BF16), transposed=True, cc=cc)

    head_ids = jnp.arange(1, N_HEADS + 1, dtype=F32)
    slopes = (jnp.exp2(-8.0 * head_ids / N_HEADS) * LOG2E).reshape(N_GROUPS, 1, HEADS_PER_GROUP, 1)
    slopes = jnp.broadcast_to(slopes, (N_GROUPS, 1, HEADS_PER_GROUP, Q_TILE)).reshape(
        N_GROUPS, 1, HEADS_PER_GROUP * Q_TILE)
    o_nsa = _nsa_call(proj, kc, vct, vst, vwt, gates, slopes, t=t,
                      ks_col=(qw + 2 * kvw) // HEAD_DIM, kw_col=(qw + 3 * kvw) // HEAD_DIM)

    u2 = _conv_front_call(gm, p["conv_dw_w"], p["conv_dw_b"], p["conv_ln_g"], p["conv_ln_b"])

    tn = 1024
    nb = d // tn
    ya = _matmul_call(o_nsa, p["w_o_nsa"].astype(BF16), tm=512, tn=tn, out_dtype=F32, name="o_proj",
                      epilogue=lambda acc, ga_ref: acc * _sigmoid(ga_ref[...].astype(F32)),
                      extras=(gm,), extra_specs=(pl.BlockSpec((512, tn), lambda i, j: (i, 2 * nb + j)),))
    merged = _matmul_call(
        u2, p["conv_pw_w"].astype(BF16), tm=512, tn=tn, out_dtype=BF16, name="pw_merge",
        epilogue=lambda acc, b_ref, gb_ref, ya_ref: (acc + b_ref[...]) * _sigmoid(gb_ref[...].astype(F32)) + ya_ref[...],
        extras=(p["conv_pw_b"].reshape(1, d), gm, ya),
        extra_specs=(pl.BlockSpec((1, tn), lambda i, j: (0, j)),
                     pl.BlockSpec((512, tn), lambda i, j: (i, 3 * nb + j)),
                     pl.BlockSpec((512, tn), lambda i, j: (i, j))))
    x1 = _matmul_call(
        merged, p["w_out"].astype(BF16), tm=512, tn=tn, out_dtype=F32, name="out_proj",
        epilogue=lambda acc, x_ref, g_ref: x_ref[...] + g_ref[0:1, :] * acc,
        extras=(x, ada),
        extra_specs=(pl.BlockSpec((512, tn), lambda i, j: (i, j)),
                     pl.BlockSpec((8, tn), lambda i, j: (0, 2 * nb + j))))

    h2 = _norm_call(x1, p["norm2_g"], ada, shift_idx=3, scale_idx=4)
    act = _ffn_up_call(h2, p["ffn_w_up"].astype(BF16), p["ffn_dw_w"], p["ffn_dw_b"])
    x2 = _matmul_call(
        act, p["ffn_w_down"].astype(BF16), tm=512, tn=512, out_dtype=F32, name="ffn_down",
        epilogue=lambda acc, x_ref, g_ref: x_ref[...] + g_ref[0:1, :] * acc,
        extras=(x1, ada),
        extra_specs=(pl.BlockSpec((512, 512), lambda i, j: (i, j)),
                     pl.BlockSpec((8, 512), lambda i, j: (0, 5 * (d // 512) + j))))
    return x2


def kernel(x, c, w_ada, b_ada, norm1_g, w_in, cmp_pe, w_kc1, w_kc2, w_vc1, w_vc2, w_o_nsa, conv_dw_w, conv_dw_b, conv_ln_g, conv_ln_b, conv_pw_w, conv_pw_b, w_out, norm2_g, ffn_w_up, ffn_dw_w, ffn_dw_b, ffn_w_down, final_g):
    b, t, d = x.shape
    stacked = dict(w_ada=w_ada, b_ada=b_ada, norm1_g=norm1_g, w_in=w_in, cmp_pe=cmp_pe, w_kc1=w_kc1,
                   w_kc2=w_kc2, w_vc1=w_vc1, w_vc2=w_vc2, w_o_nsa=w_o_nsa, conv_dw_w=conv_dw_w,
                   conv_dw_b=conv_dw_b, conv_ln_g=conv_ln_g, conv_ln_b=conv_ln_b, conv_pw_w=conv_pw_w,
                   conv_pw_b=conv_pw_b, w_out=w_out, norm2_g=norm2_g, ffn_w_up=ffn_w_up,
                   ffn_dw_w=ffn_dw_w, ffn_dw_b=ffn_dw_b, ffn_w_down=ffn_w_down)
    depth = w_ada.shape[0]
    outs = []
    for bi in range(b):
        xb = x[bi]
        for layer in range(depth):
            xb = _layer(xb, c[bi:bi + 1], {k: v[layer] for k, v in stacked.items()})
        outs.append(_norm_call(xb, final_g, out_dtype=x.dtype))
    return jnp.stack(outs)
```

```python
import functools

import jax
import jax.numpy as jnp
from jax import lax
from jax.experimental import pallas as pl
from jax.experimental.pallas import tpu as pltpu

F32 = jnp.float32
BF16 = jnp.bfloat16

N_HEADS = 16
N_GROUPS = 2
HEADS_PER_GROUP = N_HEADS // N_GROUPS
HEAD_DIM = 128
CMP_BLOCK = 32
CMP_STRIDE = 16
CMP_HIDDEN = 2 * HEAD_DIM
SEL_BLOCK = 64
SEL_TOP_N = 16
WINDOW = 512
FORCE_BONUS = 1e6
CONV_WIDTH = 31
FFN_CONV_WIDTH = 3
EPS = 1e-6
NEG_INF = -1e30
LOG2E = 1.4426950408889634

V7X_VMEM_BYTES = 64 * 1024 * 1024
VMEM_LIMIT_BYTES = V7X_VMEM_BYTES - 8 * 1024 * 1024
LANES = 128
F32_SUBLANES = 8

Q_TILE = 128
SLC_CHUNK = 512
WIN_CHUNK = 128
CMP_CHUNK = 256
FEAT_ONEHOT = 16
MASK_BIG = 2.0 ** 20
N_FORCED = 3


def _cparams(*sem):
    return pltpu.CompilerParams(dimension_semantics=sem, vmem_limit_bytes=VMEM_LIMIT_BYTES)


def _sigmoid(v):
    return 1.0 / (1.0 + jnp.exp(-v))


def _nt_dot(a, b):
    return lax.dot_general(a, b, (((1,), (1,)), ((), ())), preferred_element_type=F32)


def _ada_call(c_row, w, b):
    d, n = w.shape
    tn = 1024
    cb = jnp.broadcast_to(c_row.reshape(d, 1), (d, LANES))

    def body(cb_ref, w_ref, b_ref, o_ref):
        cv = cb_ref[...]
        act = cv * _sigmoid(cv)
        for j in range(tn // LANES):
            sl = slice(j * LANES, (j + 1) * LANES)
            row = jnp.sum(w_ref[:, sl] * act, axis=0, keepdims=True) + b_ref[:, sl]
            o_ref[:, sl] = jnp.broadcast_to(row, (8, LANES))

    return pl.pallas_call(
        body,
        grid=(n // tn,),
        in_specs=[pl.BlockSpec((d, LANES), lambda j: (0, 0)),
                  pl.BlockSpec((d, tn), lambda j: (0, j)),
                  pl.BlockSpec((1, tn), lambda j: (0, j))],
        out_specs=pl.BlockSpec((8, tn), lambda j: (0, j)),
        out_shape=jax.ShapeDtypeStruct((8, n), F32),
        compiler_params=_cparams("parallel"),
        name="ada",
    )(cb, w, b.reshape(1, n))


def _norm_call(x, gain, ada=None, shift_idx=0, scale_idx=0, out_dtype=BF16):
    t, d = x.shape
    tm = 512

    def body(*refs):
        if ada is None:
            x_ref, g_ref, o_ref = refs
        else:
            x_ref, g_ref, sh_ref, sc_ref, o_ref = refs
        xv = x_ref[...]
        y = xv * lax.rsqrt(jnp.mean(xv * xv, axis=-1, keepdims=True) + EPS) * g_ref[...]
        if ada is not None:
            y = y * (1.0 + sc_ref[0:1, :]) + sh_ref[0:1, :]
        o_ref[...] = y.astype(o_ref.dtype)

    in_specs = [pl.BlockSpec((tm, d), lambda i: (i, 0)), pl.BlockSpec((1, d), lambda i: (0, 0))]
    args = [x, gain.reshape(1, d)]
    if ada is not None:
        in_specs += [pl.BlockSpec((8, d), lambda i: (0, shift_idx)),
                     pl.BlockSpec((8, d), lambda i: (0, scale_idx))]
        args += [ada, ada]
    return pl.pallas_call(
        body,
        grid=(t // tm,),
        in_specs=in_specs,
        out_specs=pl.BlockSpec((tm, d), lambda i: (i, 0)),
        out_shape=jax.ShapeDtypeStruct((t, d), out_dtype),
        compiler_params=_cparams("parallel"),
        name="norm",
    )(*args)


def _matmul_call(a, w, *, tm, tn, out_dtype, name, epilogue=None, extras=(), extra_specs=()):
    m, k = a.shape
    n = w.shape[1]
    tm = min(tm, m)

    def body(a_ref, w_ref, *rest):
        o_ref = rest[-1]
        acc = jnp.dot(a_ref[...], w_ref[...].astype(BF16), preferred_element_type=F32)
        if epilogue is not None:
            acc = epilogue(acc, *rest[:-1])
        o_ref[...] = acc.astype(o_ref.dtype)

    return pl.pallas_call(
        body,
        grid=(m // tm, n // tn),
        in_specs=[pl.BlockSpec((tm, k), lambda i, j: (i, 0)),
                  pl.BlockSpec((k, tn), lambda i, j: (0, j))] + list(extra_specs),
        out_specs=pl.BlockSpec((tm, tn), lambda i, j: (i, j)),
        out_shape=jax.ShapeDtypeStruct((m, n), out_dtype),
        compiler_params=_cparams("parallel", "parallel"),
        name=name,
    )(a, w, *extras)


def _nt_proj_call(h, w_qt, w_vt, w_gt, q_scale):
    t, d = h.shape
    tq = SLC_CHUNK
    nqr = w_qt.shape[0]
    nv = w_vt.shape[0] // 2
    ng = w_gt.shape[0]
    per = tq // WIN_CHUNK

    def body(h_ref, wq_ref, wv_ref, wg_ref, q_ref, vs_ref, vw_ref, g_ref):
        hv = h_ref[...]
        q_ref[...] = (_nt_dot(wq_ref[...], hv) * q_scale).astype(BF16)
        r = _nt_dot(wv_ref[...], hv)
        vs_ref[0] = r[0:nv].astype(BF16)
        for kk in range(per):
            vw_ref[kk] = r[nv:2 * nv, kk * WIN_CHUNK:(kk + 1) * WIN_CHUNK].astype(BF16)
        g_ref[...] = _sigmoid(_nt_dot(wg_ref[...], hv))

    return pl.pallas_call(
        body,
        grid=(t // tq,),
        in_specs=[pl.BlockSpec((tq, d), lambda i: (i, 0)),
                  pl.BlockSpec((nqr, d), lambda i: (0, 0)),
                  pl.BlockSpec((2 * nv, d), lambda i: (0, 0)),
                  pl.BlockSpec((ng, d), lambda i: (0, 0))],
        out_specs=[pl.BlockSpec((nqr, tq), lambda i: (0, i)),
                   pl.BlockSpec((1, nv, tq), lambda i: (i, 0, 0)),
                   pl.BlockSpec((per, nv, WIN_CHUNK), lambda i: (i, 0, 0)),
                   pl.BlockSpec((ng, tq), lambda i: (0, i))],
        out_shape=[jax.ShapeDtypeStruct((nqr, t), BF16),
                   jax.ShapeDtypeStruct((t // tq, nv, tq), BF16),
                   jax.ShapeDtypeStruct((t // WIN_CHUNK, nv, WIN_CHUNK), BF16),
                   jax.ShapeDtypeStruct((ng, t), F32)],
        compiler_params=_cparams("parallel"),
        name="nt_proj",
    )(h, w_qt, w_vt, w_gt)


def _compress_call(xc, wa, wb, pe8, w2, *, transposed, cc):
    nc, kk = xc.shape

    def body(x_ref, wa_ref, wb_ref, pe_ref, w2_ref, o_ref):
        xv = x_ref[...]
        top = jnp.dot(xv, wa_ref[...], preferred_element_type=F32)
        bot = jnp.dot(xv, wb_ref[...], preferred_element_type=F32)
        pe_top = jnp.dot(pe_ref[...], wa_ref[...], preferred_element_type=F32)
        pe_bot = jnp.dot(pe_ref[...], wb_ref[...], preferred_element_type=F32)
        pre = top + pltpu.roll(bot, nc - 1, 0) + pe_top[0:1] + pe_bot[1:2]
        hid = (pre * _sigmoid(pre)).astype(BF16)
        if transposed:
            r = _nt_dot(w2_ref[...], hid)
            for c in range(nc // cc):
                o_ref[c] = r[:, c * cc:(c + 1) * cc].astype(BF16)
        else:
            o_ref[...] = jnp.dot(hid, w2_ref[...], preferred_element_type=F32).astype(BF16)

    if transposed:
        out_spec = pl.BlockSpec((None, nc // cc, HEAD_DIM, cc), lambda g: (g, 0, 0, 0))
        out_shape = jax.ShapeDtypeStruct((N_GROUPS, nc // cc, HEAD_DIM, cc), BF16)
    else:
        out_spec = pl.BlockSpec((None, nc, HEAD_DIM), lambda g: (g, 0, 0))
        out_shape = jax.ShapeDtypeStruct((N_GROUPS, nc, HEAD_DIM), BF16)
    return pl.pallas_call(
        body,
        grid=(N_GROUPS,),
        in_specs=[pl.BlockSpec((nc, kk), lambda g: (0, 0)),
                  pl.BlockSpec((None, kk, CMP_HIDDEN), lambda g: (g, 0, 0)),
                  pl.BlockSpec((None, kk, CMP_HIDDEN), lambda g: (g, 0, 0)),
                  pl.BlockSpec((8, kk), lambda g: (0, 0)),
                  pl.BlockSpec(w2.shape, lambda g: (0, 0))],
        out_specs=out_spec,
        out_shape=out_shape,
        compiler_params=_cparams("parallel"),
        name="compress_v" if transposed else "compress_k",
    )(xc, wa, wb, pe8, w2)


def _nsa_feature_tables(nc):
    def table(nrows, pos_step, coarse):
        n = jnp.arange(nrows, dtype=jnp.int32)
        pos = n * pos_step
        hi = (pos // coarse) * coarse
        lo = pos - hi
        cols = [hi, hi, hi, lo, lo, lo] + [jnp.zeros_like(n)] * (FEAT_ONEHOT - 6)
        cols += [(n // SEL_BLOCK == b).astype(jnp.int32) for b in range(SLC_CHUNK // SEL_BLOCK)]
        cols += [jnp.zeros_like(n)] * (HEAD_DIM - len(cols))
        return jnp.stack(cols, axis=1).astype(BF16)

    slc = table(SLC_CHUNK, 1, 16)
    win = table(WINDOW + Q_TILE, 1, 16).at[:, FEAT_ONEHOT:].set(0)
    cmp_ = table(min(CMP_CHUNK, nc), CMP_STRIDE, 256).at[:, FEAT_ONEHOT:].set(0)
    return slc, win, cmp_


def _nsa_call(qt, proj, kc, vct, vst, vwt, gates, slopes, qfeat, *, t, ks_col, kw_col):
    nq = t // Q_TILE
    n_sel = t // SEL_BLOCK
    nc = t // CMP_STRIDE
    cc = min(CMP_CHUNK, nc)
    top_n = min(SEL_TOP_N, n_sel)
    hg = HEADS_PER_GROUP
    wide = hg * Q_TILE
    sel_per_chunk = SLC_CHUNK // SEL_BLOCK
    win_span = WINDOW + Q_TILE
    kdim = 2 * HEAD_DIM
    tab_slc, tab_win, tab_cmp = _nsa_feature_tables(nc)

    def body(q_ref, kc_ref, vct_ref, ks_ref, vst_ref, kw_ref, vwt_ref, g_ref, sl_ref, qf_ref,
             ts_ref, tw_ref, tc_ref, o_ref,
             qt_ref, kss_ref, ksw_ref, ksc_ref, m_ref, l_ref, acc_ref, p_ref, mw_ref, lw_ref, accw_ref, pw_ref,
             out_ref, s_ref, ps_ref, score_ref, sel_ref, flag_ref):
        g = pl.program_id(0)
        i = pl.program_id(1)
        t0 = i * Q_TILE
        tq = t0 + lax.broadcasted_iota(jnp.int32, (1, Q_TILE), 1)

        for h in range(hg):
            qt_ref[0:HEAD_DIM, h * Q_TILE:(h + 1) * Q_TILE] = q_ref[h * HEAD_DIM:(h + 1) * HEAD_DIM, :]
        qt_ref[HEAD_DIM:HEAD_DIM + FEAT_ONEHOT, :] = qf_ref[...]
        qt_ref[HEAD_DIM + FEAT_ONEHOT:kdim, :] = jnp.zeros((HEAD_DIM - FEAT_ONEHOT, wide), BF16)
        kss_ref[:, HEAD_DIM:kdim] = ts_ref[...]
        ksw_ref[:, HEAD_DIM:kdim] = tw_ref[...]
        ksc_ref[:, HEAD_DIM:kdim] = tc_ref[...]

        state_main = (m_ref, l_ref, acc_ref, p_ref)
        state_win = (mw_ref, lw_ref, accw_ref, pw_ref)

        def reset(state):
            state[0][...] = jnp.full((1, wide), NEG_INF, F32)
            state[1][...] = jnp.zeros((1, wide), F32)
            state[2][...] = jnp.zeros((HEAD_DIM, wide), F32)

        def chunk_offset(ref_pos):
            return sl_ref[...] * (ref_pos - t0).astype(F32)

        def attend(state, kst_ref, vt_c, off, nrows, mask=None):
            m_ref, l_ref, acc_ref, p_ref = state
            hh = hg // 2
            kst = kst_ref[0:nrows, :]
            sts = [jnp.dot(kst, qt_ref[:, half * hh * Q_TILE:(half + 1) * hh * Q_TILE],
                           preferred_element_type=F32) for half in range(2)]
            for half in range(2):
                lanes = slice(half * hh * Q_TILE, (half + 1) * hh * Q_TILE)
                st = sts[half]
                alphas = []
                for k in range(hh):
                    sl = slice((half * hh + k) * Q_TILE, (half * hh + k + 1) * Q_TILE)
                    s = st[:, k * Q_TILE:(k + 1) * Q_TILE]
                    if mask is not None:
                        s = jnp.where(mask, s, NEG_INF)
                    o = off[:, sl]
                    m_old = m_ref[:, sl]
                    m_new = jnp.maximum(m_old, jnp.max(s, axis=0, keepdims=True) + o)
                    alpha = jnp.exp2(m_old - m_new)
                    p = jnp.exp2(s - (m_new - o))
                    l_ref[:, sl] = alpha * l_ref[:, sl] + jnp.sum(p, axis=0, keepdims=True)
                    m_ref[:, sl] = m_new
                    alphas.append(alpha)
                    p_ref[0:nrows, sl] = p.astype(BF16)
                pv = jnp.dot(vt_c, p_ref[0:nrows, lanes], preferred_element_type=F32)
                acc_ref[:, lanes] = acc_ref[:, lanes] * jnp.concatenate(alphas, axis=1) + pv

        def inv_norm(state, sl):
            valid = state[0][:, sl] > 0.5 * NEG_INF
            return jnp.where(valid, 1.0 / jnp.maximum(state[1][:, sl], 1e-30), 0.0)

        def finish(state, branch, first):
            for h in range(hg):
                sl = slice(h * Q_TILE, (h + 1) * Q_TILE)
                gate = g_ref[pl.ds(branch * N_HEADS + g * hg + h, 1), :]
                contrib = state[2][:, sl] * (inv_norm(state, sl) * gate)
                if first:
                    out_ref[:, sl] = contrib
                else:
                    out_ref[:, sl] += contrib

        reset(state_main)
        ps_ref[...] = jnp.zeros(ps_ref.shape, F32)
        n_cmp_keys = (t0 + Q_TILE - CMP_BLOCK) // CMP_STRIDE + 1
        n_cc = (n_cmp_keys + cc - 1) // cc

        def cmp_scores(c, carry):
            r0 = pl.multiple_of(c * cc, cc)
            n = r0 + lax.broadcasted_iota(jnp.int32, (cc, Q_TILE), 0)
            mask = (n * CMP_STRIDE + (CMP_BLOCK - 1)) <= tq
            off = chunk_offset(r0 * CMP_STRIDE + (CMP_BLOCK - 1))
            ksc_ref[:, 0:HEAD_DIM] = kc_ref[pl.ds(r0, cc), :]
            for hp in range(hg // 2):
                pair = slice(hp * 2 * Q_TILE, (hp + 1) * 2 * Q_TILE)
                st = jnp.dot(ksc_ref[...], qt_ref[:, pair], preferred_element_type=F32)
                for hh in range(2):
                    sl = slice((2 * hp + hh) * Q_TILE, (2 * hp + hh + 1) * Q_TILE)
                    s = jnp.where(mask, st[:, hh * Q_TILE:(hh + 1) * Q_TILE] + off[:, sl], NEG_INF)
                    s_ref[pl.ds(r0, cc), sl] = s
                    m_ref[:, sl] = jnp.maximum(m_ref[:, sl], jnp.max(s, axis=0, keepdims=True))
            return carry

        lax.fori_loop(0, n_cc, cmp_scores, 0)

        def cmp_probs(c, carry):
            r0 = pl.multiple_of(c * cc, cc)
            for hp in range(hg // 2):
                pair = slice(hp * 2 * Q_TILE, (hp + 1) * 2 * Q_TILE)
                for hh in range(2):
                    sl = slice((2 * hp + hh) * Q_TILE, (2 * hp + hh + 1) * Q_TILE)
                    p = jnp.exp2(s_ref[pl.ds(r0, cc), sl] - m_ref[:, sl])
                    l_ref[:, sl] += jnp.sum(p, axis=0, keepdims=True)
                    s_ref[pl.ds(r0, cc), sl] = p
                    p_ref[0:cc, sl] = p.astype(BF16)
                acc_ref[:, pair] += jnp.dot(vct_ref[c], p_ref[0:cc, pair], preferred_element_type=F32)
            return carry

        lax.fori_loop(0, n_cc, cmp_probs, 0)
        finish(state_main, 0, True)

        def ps_body(c, carry):
            r0 = pl.multiple_of(c * cc, cc)
            tot = jnp.zeros((cc, Q_TILE), F32)
            for h in range(hg):
                sl = slice(h * Q_TILE, (h + 1) * Q_TILE)
                tot = tot + s_ref[pl.ds(r0, cc), sl] * inv_norm(state_main, sl)
            ps_ref[pl.ds(8 + r0, cc), :] = tot
            return carry

        lax.fori_loop(0, n_cc, ps_body, 0)

        ratio = SEL_BLOCK // CMP_STRIDE
        imp = ps_ref[pl.ds(7, n_sel, stride=ratio), :]
        for kk in range(ratio):
            imp = imp + ps_ref[pl.ds(8 + kk, n_sel, stride=ratio), :]
        jrow = lax.broadcasted_iota(jnp.int32, (n_sel, Q_TILE), 0)
        cur = tq // SEL_BLOCK
        forced = jnp.where(jrow == 0, 1.0, jnp.where(jrow == cur, 1.0, jnp.where(jrow == cur - 1, 1.0, 0.0)))
        score_ref[...] = jnp.where(forced > 0.5, -2.0, jnp.where(jrow <= cur, imp, -1.0))
        sel_ref[...] = forced
        jrow_f = jrow.astype(F32)

        def topk_body(r, carry):
            sc = score_ref[...]
            mx = jnp.max(sc, axis=0, keepdims=True)
            first = jnp.min(jnp.where(sc == mx, jrow_f, float(n_sel)), axis=0, keepdims=True)
            pick = jrow_f == first
            sel_ref[...] = jnp.where(pick, 1.0, sel_ref[...])
            score_ref[...] = jnp.where(pick, -2.0, sc)
            return carry

        lax.fori_loop(0, top_n - N_FORCED, topk_body, 0)
        for c in range(t // SLC_CHUNK):
            blocks = sel_ref[c * sel_per_chunk:(c + 1) * sel_per_chunk, :]
            flag_ref[c] = (jnp.max(blocks) > 0.5).astype(jnp.int32)

        reset(state_main)
        last_sc = (t0 + Q_TILE - 1) // SLC_CHUNK

        def slc_chunk(c, causal):
            r0 = pl.multiple_of(c * SLC_CHUNK, SLC_CHUNK)
            kss_ref[:, 0:HEAD_DIM] = ks_ref[pl.ds(r0, SLC_CHUNK), :]
            sel8 = sel_ref[pl.ds(pl.multiple_of(c * sel_per_chunk, sel_per_chunk), sel_per_chunk), :]
            bias = (sel8 - 1.0) * MASK_BIG
            rows = jnp.concatenate([jnp.concatenate([bias] * hg, axis=1),
                                    jnp.zeros((FEAT_ONEHOT - sel_per_chunk, wide), F32)], axis=0)
            qt_ref[HEAD_DIM + FEAT_ONEHOT:HEAD_DIM + 2 * FEAT_ONEHOT, :] = rows.astype(BF16)
            mask = None
            if causal:
                mask = (r0 + lax.broadcasted_iota(jnp.int32, (SLC_CHUNK, Q_TILE), 0)) <= tq
            attend(state_main, kss_ref, vst_ref[c], chunk_offset(r0), SLC_CHUNK, mask)

        def slc_body(c, carry):
            @pl.when(flag_ref[c] > 0)
            def _():
                slc_chunk(c, False)

            return carry

        lax.fori_loop(0, last_sc, slc_body, 0)
        slc_chunk(last_sc, True)
        finish(state_main, 1, False)

        reset(state_win)
        n_wc = win_span // WIN_CHUNK
        c0 = jnp.maximum(i * (Q_TILE // WIN_CHUNK) - WINDOW // WIN_CHUNK, 0)
        r0w = pl.multiple_of(c0 * WIN_CHUNK, WIN_CHUNK)
        dposw = (r0w + lax.broadcasted_iota(jnp.int32, (win_span, Q_TILE), 0)) - tq
        maskw = jnp.where(dposw <= 0, dposw, -2 * WINDOW) > -WINDOW
        vtw = jnp.concatenate([vwt_ref[c0 + kk] for kk in range(n_wc)], axis=1)
        ksw_ref[:, 0:HEAD_DIM] = kw_ref[pl.ds(r0w, win_span), :]
        attend(state_win, ksw_ref, vtw, chunk_offset(r0w), win_span, maskw)
        finish(state_win, 2, False)

        for h in range(hg):
            sl = slice(h * Q_TILE, (h + 1) * Q_TILE)
            o_ref[:, h * HEAD_DIM:(h + 1) * HEAD_DIM] = out_ref[:, sl].T.astype(o_ref.dtype)

    one = pl.Buffered(1)
    in_specs = [
        pl.BlockSpec((hg * HEAD_DIM, Q_TILE), lambda g, i: (g, i)),
        pl.BlockSpec((None, nc, HEAD_DIM), lambda g, i: (g, 0, 0)),
        pl.BlockSpec((None, nc // cc, HEAD_DIM, cc), lambda g, i: (g, 0, 0, 0)),
        pl.BlockSpec((t, HEAD_DIM), lambda g, i: (0, ks_col + g), pipeline_mode=one),
        pl.BlockSpec((t // SLC_CHUNK, HEAD_DIM, SLC_CHUNK), lambda g, i: (0, g, 0), pipeline_mode=one),
        pl.BlockSpec((t, HEAD_DIM), lambda g, i: (0, kw_col + g), pipeline_mode=one),
        pl.BlockSpec((t // WIN_CHUNK, HEAD_DIM, WIN_CHUNK), lambda g, i: (0, g, 0), pipeline_mode=one),
        pl.BlockSpec((gates.shape[0], Q_TILE), lambda g, i: (0, i)),
        pl.BlockSpec((None, 1, wide), lambda g, i: (g, 0, 0)),
        pl.BlockSpec((None, FEAT_ONEHOT, wide), lambda g, i: (g, 0, 0)),
        pl.BlockSpec(tab_slc.shape, lambda g, i: (0, 0)),
        pl.BlockSpec(tab_win.shape, lambda g, i: (0, 0)),
        pl.BlockSpec(tab_cmp.shape, lambda g, i: (0, 0)),
    ]
    scratch = [
        pltpu.VMEM((kdim, wide), BF16),
        pltpu.VMEM((SLC_CHUNK, kdim), BF16),
        pltpu.VMEM((win_span, kdim), BF16),
        pltpu.VMEM((cc, kdim), BF16),
        pltpu.VMEM((1, wide), F32),
        pltpu.VMEM((1, wide), F32),
        pltpu.VMEM((HEAD_DIM, wide), F32),
        pltpu.VMEM((SLC_CHUNK, wide), BF16),
        pltpu.VMEM((1, wide), F32),
        pltpu.VMEM((1, wide), F32),
        pltpu.VMEM((HEAD_DIM, wide), F32),
        pltpu.VMEM((win_span, wide), BF16),
        pltpu.VMEM((HEAD_DIM, wide), F32),
        pltpu.VMEM((nc, wide), F32),
        pltpu.VMEM((8 + nc, Q_TILE), F32),
        pltpu.VMEM((n_sel, Q_TILE), F32),
        pltpu.VMEM((n_sel, Q_TILE), F32),
        pltpu.SMEM((t // SLC_CHUNK,), jnp.int32),
    ]
    return pl.pallas_call(
        body,
        grid=(N_GROUPS, nq),
        in_specs=in_specs,
        out_specs=pl.BlockSpec((Q_TILE, hg * HEAD_DIM), lambda g, i: (i, g)),
        out_shape=jax.ShapeDtypeStruct((t, N_HEADS * HEAD_DIM), BF16),
        scratch_shapes=scratch,
        compiler_params=_cparams("parallel", "parallel"),
        name="nsa",
    )(qt, kc, vct, proj, vst, proj, vwt, gates, slopes, qfeat, tab_slc, tab_win, tab_cmp)


def _conv_front_call(gm, w_dw, b_dw, ln_g, ln_b):
    t = gm.shape[0]
    ch = w_dw.shape[1]
    tm = min(256, t)
    halo = 32
    rb = 32
    strip = 512
    w_rep = jnp.repeat(w_dw, F32_SUBLANES, axis=0)

    def body(a_ref, g_ref, ah_ref, gh_ref, w_ref, b_ref, lg_ref, lb_ref, o_ref, u_ref, sb_ref, y_ref):
        i = pl.program_id(0)
        av = a_ref[...].astype(F32)
        gv = g_ref[...].astype(F32)
        u_ref[halo:halo + tm, :] = av * _sigmoid(gv)
        ahv = ah_ref[...].astype(F32)
        ghv = gh_ref[...].astype(F32)
        u_ref[0:halo, :] = jnp.where(i > 0, ahv * _sigmoid(ghv), 0.0)
        for cs in range(ch // strip):
            cols = slice(cs * strip, (cs + 1) * strip)
            ue = u_ref[:, cols]
            sb_ref[0] = ue
            for part in range(1, F32_SUBLANES):
                sb_ref[part] = pltpu.roll(ue, part, 0)

            def rows(r, carry, cols=cols):
                r0 = pl.multiple_of(r * rb, rb)
                acc = jnp.zeros((rb // F32_SUBLANES, F32_SUBLANES, strip), F32)
                for s in range(CONV_WIDTH):
                    whole, part = divmod(s, F32_SUBLANES)
                    k = CONV_WIDTH - 1 - s
                    tap = sb_ref[part, pl.ds(halo + r0 - F32_SUBLANES * whole, rb), :]
                    wk = w_ref[k * F32_SUBLANES:(k + 1) * F32_SUBLANES, cols]
                    acc = acc + tap.reshape(rb // F32_SUBLANES, F32_SUBLANES, strip) * wk
                y_ref[pl.ds(r0, rb), cols] = acc.reshape(rb, strip) + b_ref[:, cols]
                return carry

            lax.fori_loop(0, tm // rb, rows, 0)
        y = y_ref[...]
        mu = jnp.mean(y, axis=-1, keepdims=True)
        yc = y - mu
        var = jnp.mean(yc * yc, axis=-1, keepdims=True)
        z = yc * lax.rsqrt(var + EPS) * lg_ref[...] + lb_ref[...]
        o_ref[...] = (z * _sigmoid(z)).astype(o_ref.dtype)

    per = tm // halo
    return pl.pallas_call(
        body,
        grid=(t // tm,),
        in_specs=[pl.BlockSpec((tm, ch), lambda i: (i, 0)),
                  pl.BlockSpec((tm, ch), lambda i: (i, 1)),
                  pl.BlockSpec((halo, ch), lambda i: (jnp.maximum(i * per - 1, 0), 0)),
                  pl.BlockSpec((halo, ch), lambda i: (jnp.maximum(i * per - 1, 0), 1)),
                  pl.BlockSpec((CONV_WIDTH * F32_SUBLANES, ch), lambda i: (0, 0)),
                  pl.BlockSpec((1, ch), lambda i: (0, 0)),
                  pl.BlockSpec((1, ch), lambda i: (0, 0)),
                  pl.BlockSpec((1, ch), lambda i: (0, 0))],
        out_specs=pl.BlockSpec((tm, ch), lambda i: (i, 0)),
        out_shape=jax.ShapeDtypeStruct((t, ch), BF16),
        scratch_shapes=[pltpu.VMEM((halo + tm, ch), F32),
                        pltpu.VMEM((F32_SUBLANES, halo + tm, strip), F32),
                        pltpu.VMEM((tm, ch), F32)],
        compiler_params=_cparams("parallel"),
        name="conv_front",
    )(gm, gm, gm, gm, w_rep, b_dw.reshape(1, ch), ln_g.reshape(1, ch), ln_b.reshape(1, ch))


def _ffn_up_call(h2, w_up, w_dw, b_dw):
    t, d = h2.shape
    n2 = w_up.shape[1]
    f = n2 // 2
    tm = min(1024, t)
    rsub = 256
    tn = 512
    nj = f // tn
    sub = F32_SUBLANES
    w_pad = jnp.concatenate([w_dw, jnp.zeros((sub - FFN_CONV_WIDTH, n2), F32)], axis=0)
    b2 = b_dw.reshape(1, n2)

    def body(h_ref, wa_ref, wg_ref, da_ref, dg_ref, ba_ref, bg_ref, o_ref, ca_ref, cg_ref):
        i = pl.program_id(0)
        j = pl.program_id(1)

        @pl.when(i == 0)
        def _():
            ca_ref[j] = jnp.zeros((sub, tn), F32)
            cg_ref[j] = jnp.zeros((sub, tn), F32)

        wa = wa_ref[...].astype(BF16)
        wg = wg_ref[...].astype(BF16)

        def conv(up, prev, d_ref, b_ref):
            ext = jnp.concatenate([prev, up], axis=0)
            acc = b_ref[...] + up * d_ref[FFN_CONV_WIDTH - 1:FFN_CONV_WIDTH, :]
            for s in range(1, FFN_CONV_WIDTH):
                k = FFN_CONV_WIDTH - 1 - s
                acc = acc + pltpu.roll(ext, s, 0)[sub:, :] * d_ref[k:k + 1, :]
            return acc

        prev_a = ca_ref[j]
        prev_g = cg_ref[j]
        for rs in range(tm // rsub):
            rows = slice(rs * rsub, (rs + 1) * rsub)
            hv = h_ref[rows, :]
            up_a = jnp.dot(hv, wa, preferred_element_type=F32)
            up_g = jnp.dot(hv, wg, preferred_element_type=F32)
            ca = conv(up_a, prev_a, da_ref, ba_ref)
            cg = conv(up_g, prev_g, dg_ref, bg_ref)
            o_ref[rows, :] = (cg * _sigmoid(cg) * ca).astype(o_ref.dtype)
            prev_a = up_a[rsub - sub:, :]
            prev_g = up_g[rsub - sub:, :]
        ca_ref[j] = prev_a
        cg_ref[j] = prev_g

    return pl.pallas_call(
        body,
        grid=(t // tm, nj),
        in_specs=[pl.BlockSpec((tm, d), lambda i, j: (i, 0)),
                  pl.BlockSpec((d, tn), lambda i, j: (0, j)),
                  pl.BlockSpec((d, tn), lambda i, j: (0, nj + j)),
                  pl.BlockSpec((sub, tn), lambda i, j: (0, j)),
                  pl.BlockSpec((sub, tn), lambda i, j: (0, nj + j)),
                  pl.BlockSpec((1, tn), lambda i, j: (0, j)),
                  pl.BlockSpec((1, tn), lambda i, j: (0, nj + j))],
        out_specs=pl.BlockSpec((tm, tn), lambda i, j: (i, j)),
        out_shape=jax.ShapeDtypeStruct((t, f), BF16),
        scratch_shapes=[pltpu.VMEM((nj, sub, tn), F32), pltpu.VMEM((nj, sub, tn), F32)],
        compiler_params=_cparams("arbitrary", "arbitrary"),
        name="ffn_up",
    )(h2, w_up, w_up, w_pad, w_pad, b2, b2)


def _layer(x, c_row, p):
    t, d = x.shape
    ada = _ada_call(c_row, p["w_ada"], p["b_ada"])

    w_in = p["w_in"]
    qw = N_HEADS * HEAD_DIM
    kvw = N_GROUPS * HEAD_DIM
    o = [0, qw]
    for _ in range(6):
        o.append(o[-1] + kvw)
    o.append(o[-1] + 3 * N_HEADS)
    o.append(o[-1] + 2 * d)
    o.append(o[-1] + 2 * d)
    w_q, w_kc, w_vc, w_ks, w_vs, w_kw, w_vw, w_gn, w_glu, w_mg = [w_in[:, o[k]:o[k + 1]] for k in range(10)]
    w_rm = jnp.concatenate([w_kc, w_vc, w_ks, w_kw], axis=1).astype(BF16)
    w_gm = jnp.concatenate([w_glu, w_mg], axis=1).astype(BF16)
    w_qt = w_q.T.astype(BF16)
    w_vt = jnp.concatenate([w_vs, w_vw], axis=1).T.astype(BF16)
    n_gate_rows = 64
    w_gt = jnp.concatenate([w_gn.T, jnp.zeros((n_gate_rows - 3 * N_HEADS, d), F32)], axis=0).astype(BF16)

    h1 = _norm_call(x, p["norm1_g"], ada, shift_idx=0, scale_idx=1)
    proj = _matmul_call(h1, w_rm, tm=1024, tn=4 * kvw, out_dtype=BF16, name="proj_rm")
    gm = _matmul_call(h1, w_gm, tm=1024, tn=1024, out_dtype=BF16, name="proj_gm")
    qt, vst, vwt, gates = _nt_proj_call(h1, w_qt, w_vt, w_gt, HEAD_DIM ** -0.5 * LOG2E)

    nc = t // CMP_STRIDE
    cc = min(CMP_CHUNK, nc)
    half = CMP_BLOCK // CMP_STRIDE
    assert half == 2

    def expand(w1):
        w1r = w1.reshape(half, CMP_STRIDE, HEAD_DIM, CMP_HIDDEN)
        outs = []
        for hf in range(half):
            per_g = []
            for g in range(N_GROUPS):
                z = jnp.zeros((CMP_STRIDE, N_GROUPS, HEAD_DIM, CMP_HIDDEN), F32).at[:, g].set(w1r[hf])
                per_g.append(z.reshape(CMP_STRIDE * kvw, CMP_HIDDEN))
            outs.append(jnp.stack(per_g).astype(BF16))
        return outs

    pe = p["cmp_pe"].reshape(half, CMP_STRIDE, 1, HEAD_DIM)
    pe_rows = jnp.broadcast_to(pe, (half, CMP_STRIDE, N_GROUPS, HEAD_DIM)).reshape(half, CMP_STRIDE * kvw)
    pe8 = jnp.concatenate([pe_rows, jnp.zeros((8 - half, CMP_STRIDE * kvw), F32)], axis=0).astype(BF16)
    kc_raw = proj[:, 0:kvw].reshape(nc, CMP_STRIDE * kvw)
    vc_raw = proj[:, kvw:2 * kvw].reshape(nc, CMP_STRIDE * kvw)
    wka, wkb = expand(p["w_kc1"])
    wva, wvb = expand(p["w_vc1"])
    kc = _compress_call(kc_raw, wka, wkb, pe8, p["w_kc2"].astype(BF16), transposed=False, cc=cc)
    vct = _compress_call(vc_raw, wva, wvb, pe8, p["w_vc2"].T.astype(BF16), transposed=True, cc=cc)

    head_ids = jnp.arange(1, N_HEADS + 1, dtype=F32)
    slopes = (jnp.exp2(-8.0 * head_ids / N_HEADS) * LOG2E).reshape(N_GROUPS, 1, HEADS_PER_GROUP, 1)
    slopes = jnp.broadcast_to(slopes, (N_GROUPS, 1, HEADS_PER_GROUP, Q_TILE)).reshape(
        N_GROUPS, 1, HEADS_PER_GROUP * Q_TILE)
    s1 = slopes.astype(BF16)
    s2 = (slopes - s1.astype(F32)).astype(BF16)
    s3 = (slopes - s1.astype(F32) - s2.astype(F32)).astype(BF16)
    qfeat = jnp.concatenate([s1, s2, s3, s1, s2, s3,
                             jnp.zeros((N_GROUPS, FEAT_ONEHOT - 6, slopes.shape[-1]), BF16)], axis=1)
    o_nsa = _nsa_call(qt, proj, kc, vct, vst, vwt, gates, slopes, qfeat, t=t,
                      ks_col=(2 * kvw) // HEAD_DIM, kw_col=(3 * kvw) // HEAD_DIM)

    u2 = _conv_front_call(gm, p["conv_dw_w"], p["conv_dw_b"], p["conv_ln_g"], p["conv_ln_b"])

    tn = 1024
    nb = d // tn
    ya = _matmul_call(o_nsa, p["w_o_nsa"], tm=512, tn=tn, out_dtype=F32, name="o_proj",
                      epilogue=lambda acc, ga_ref: acc * _sigmoid(ga_ref[...].astype(F32)),
                      extras=(gm,), extra_specs=(pl.BlockSpec((512, tn), lambda i, j: (i, 2 * nb + j)),))
    merged = _matmul_call(
        u2, p["conv_pw_w"], tm=512, tn=tn, out_dtype=BF16, name="pw_merge",
        epilogue=lambda acc, b_ref, gb_ref, ya_ref: (acc + b_ref[...]) * _sigmoid(gb_ref[...].astype(F32)) + ya_ref[...],
        extras=(p["conv_pw_b"].reshape(1, d), gm, ya),
        extra_specs=(pl.BlockSpec((1, tn), lambda i, j: (0, j)),
                     pl.BlockSpec((512, tn), lambda i, j: (i, 3 * nb + j)),
                     pl.BlockSpec((512, tn), lambda i, j: (i, j))))
    x1 = _matmul_call(
        merged, p["w_out"], tm=512, tn=tn, out_dtype=F32, name="out_proj",
        epilogue=lambda acc, x_ref, g_ref: x_ref[...] + g_ref[0:1, :] * acc,
        extras=(x, ada),
        extra_specs=(pl.BlockSpec((512, tn), lambda i, j: (i, j)),
                     pl.BlockSpec((8, tn), lambda i, j: (0, 2 * nb + j))))

    h2 = _norm_call(x1, p["norm2_g"], ada, shift_idx=3, scale_idx=4)
    act = _ffn_up_call(h2, p["ffn_w_up"], p["ffn_dw_w"], p["ffn_dw_b"])
    x2 = _matmul_call(
        act, p["ffn_w_down"], tm=512, tn=512, out_dtype=F32, name="ffn_down",
        epilogue=lambda acc, x_ref, g_ref: x_ref[...] + g_ref[0:1, :] * acc,
        extras=(x1, ada),
        extra_specs=(pl.BlockSpec((512, 512), lambda i, j: (i, j)),
                     pl.BlockSpec((8, 512), lambda i, j: (0, 5 * (d // 512) + j))))
    return x2


def kernel(x, c, w_ada, b_ada, norm1_g, w_in, cmp_pe, w_kc1, w_kc2, w_vc1, w_vc2, w_o_nsa, conv_dw_w, conv_dw_b, conv_ln_g, conv_ln_b, conv_pw_w, conv_pw_b, w_out, norm2_g, ffn_w_up, ffn_dw_w, ffn_dw_b, ffn_w_down, final_g):
    b, t, d = x.shape
    stacked = dict(w_ada=w_ada, b_ada=b_ada, norm1_g=norm1_g, w_in=w_in, cmp_pe=cmp_pe, w_kc1=w_kc1,
                   w_kc2=w_kc2, w_vc1=w_vc1, w_vc2=w_vc2, w_o_nsa=w_o_nsa, conv_dw_w=conv_dw_w,
                   conv_dw_b=conv_dw_b, conv_ln_g=conv_ln_g, conv_ln_b=conv_ln_b, conv_pw_w=conv_pw_w,
                   conv_pw_b=conv_pw_b, w_out=w_out, norm2_g=norm2_g, ffn_w_up=ffn_w_up,
                   ffn_dw_w=ffn_dw_w, ffn_dw_b=ffn_dw_b, ffn_w_down=ffn_w_down)
    depth = w_ada.shape[0]
    outs = []
    for bi in range(b):
        xb = x[bi]
        for layer in range(depth):
            xb = _layer(xb, c[bi:bi + 1], {k: v[layer] for k, v in stacked.items()})
        outs.append(_norm_call(xb, final_g, out_dtype=x.dtype))
    return jnp.stack(outs)
```

```python
import functools

import jax
import jax.numpy as jnp
import numpy as np
from jax import lax
from jax.experimental import pallas as pl
from jax.experimental.pallas import tpu as pltpu

F32 = jnp.float32
BF16 = jnp.bfloat16

N_HEADS = 16
N_GROUPS = 2
HEADS_PER_GROUP = N_HEADS // N_GROUPS
HEAD_DIM = 128
CMP_BLOCK = 32
CMP_STRIDE = 16
CMP_HIDDEN = 2 * HEAD_DIM
SEL_BLOCK = 64
SEL_TOP_N = 16
WINDOW = 512
FORCE_BONUS = 1e6
CONV_WIDTH = 31
FFN_CONV_WIDTH = 3
EPS = 1e-6
NEG_INF = -1e30
LOG2E = 1.4426950408889634

V7X_VMEM_BYTES = 64 * 1024 * 1024
VMEM_LIMIT_BYTES = V7X_VMEM_BYTES - 8 * 1024 * 1024
LANES = 128
F32_SUBLANES = 8

Q_TILE = 128
SLC_CHUNK = 512
WIN_CHUNK = 128
CMP_CHUNK = 256
FEAT_ONEHOT = 16
MASK_BIG = 2.0 ** 20
N_FORCED = 3


def _cparams(*sem):
    return pltpu.CompilerParams(dimension_semantics=sem, vmem_limit_bytes=VMEM_LIMIT_BYTES)


def _sigmoid(v):
    return 1.0 / (1.0 + jnp.exp(-v))


def _nt_dot(a, b):
    return lax.dot_general(a, b, (((1,), (1,)), ((), ())), preferred_element_type=F32)


def _ada_call(c_row, w, b):
    d, n = w.shape
    tn = 1024
    cb = jnp.broadcast_to(c_row.reshape(d, 1), (d, LANES))

    def body(cb_ref, w_ref, b_ref, o_ref):
        cv = cb_ref[...]
        act = cv * _sigmoid(cv)
        for j in range(tn // LANES):
            sl = slice(j * LANES, (j + 1) * LANES)
            row = jnp.sum(w_ref[:, sl] * act, axis=0, keepdims=True) + b_ref[:, sl]
            o_ref[:, sl] = jnp.broadcast_to(row, (8, LANES))

    return pl.pallas_call(
        body,
        grid=(n // tn,),
        in_specs=[pl.BlockSpec((d, LANES), lambda j: (0, 0)),
                  pl.BlockSpec((d, tn), lambda j: (0, j)),
                  pl.BlockSpec((1, tn), lambda j: (0, j))],
        out_specs=pl.BlockSpec((8, tn), lambda j: (0, j)),
        out_shape=jax.ShapeDtypeStruct((8, n), F32),
        compiler_params=_cparams("parallel"),
        name="ada",
    )(cb, w, b.reshape(1, n))


def _norm_call(x, gain, ada=None, shift_idx=0, scale_idx=0, out_dtype=BF16):
    t, d = x.shape
    tm = 512

    def body(*refs):
        if ada is None:
            x_ref, g_ref, o_ref = refs
        else:
            x_ref, g_ref, sh_ref, sc_ref, o_ref = refs
        xv = x_ref[...]
        y = xv * lax.rsqrt(jnp.mean(xv * xv, axis=-1, keepdims=True) + EPS) * g_ref[...]
        if ada is not None:
            y = y * (1.0 + sc_ref[0:1, :]) + sh_ref[0:1, :]
        o_ref[...] = y.astype(o_ref.dtype)

    in_specs = [pl.BlockSpec((tm, d), lambda i: (i, 0)), pl.BlockSpec((1, d), lambda i: (0, 0))]
    args = [x, gain.reshape(1, d)]
    if ada is not None:
        in_specs += [pl.BlockSpec((8, d), lambda i: (0, shift_idx)),
                     pl.BlockSpec((8, d), lambda i: (0, scale_idx))]
        args += [ada, ada]
    return pl.pallas_call(
        body,
        grid=(t // tm,),
        in_specs=in_specs,
        out_specs=pl.BlockSpec((tm, d), lambda i: (i, 0)),
        out_shape=jax.ShapeDtypeStruct((t, d), out_dtype),
        compiler_params=_cparams("parallel"),
        name="norm",
    )(*args)


def _matmul_call(a, w, *, tm, tn, out_dtype, name, epilogue=None, extras=(), extra_specs=()):
    m, k = a.shape
    n = w.shape[1]
    tm = min(tm, m)

    def body(a_ref, w_ref, *rest):
        o_ref = rest[-1]
        acc = jnp.dot(a_ref[...], w_ref[...].astype(BF16), preferred_element_type=F32)
        if epilogue is not None:
            acc = epilogue(acc, *rest[:-1])
        o_ref[...] = acc.astype(o_ref.dtype)

    return pl.pallas_call(
        body,
        grid=(m // tm, n // tn),
        in_specs=[pl.BlockSpec((tm, k), lambda i, j: (i, 0)),
                  pl.BlockSpec((k, tn), lambda i, j: (0, j))] + list(extra_specs),
        out_specs=pl.BlockSpec((tm, tn), lambda i, j: (i, j)),
        out_shape=jax.ShapeDtypeStruct((m, n), out_dtype),
        compiler_params=_cparams("parallel", "parallel"),
        name=name,
    )(a, w, *extras)


def _out_norm_call(a, w, x, ada, gain, *, gate_idx, shift_idx, scale_idx):
    t, d = x.shape
    tm = min(512, t)

    def body(a_ref, w_ref, x_ref, g_ref, sh_ref, sc_ref, n_ref, x1_ref, h_ref):
        acc = jnp.dot(a_ref[...], w_ref[...], preferred_element_type=F32)
        x1 = x_ref[...] + g_ref[0:1, :] * acc
        x1_ref[...] = x1
        y = x1 * lax.rsqrt(jnp.mean(x1 * x1, axis=-1, keepdims=True) + EPS) * n_ref[...]
        h_ref[...] = (y * (1.0 + sc_ref[0:1, :]) + sh_ref[0:1, :]).astype(h_ref.dtype)

    return pl.pallas_call(
        body,
        grid=(t // tm,),
        in_specs=[pl.BlockSpec((tm, a.shape[1]), lambda i: (i, 0)),
                  pl.BlockSpec(w.shape, lambda i: (0, 0), pipeline_mode=pl.Buffered(1)),
                  pl.BlockSpec((tm, d), lambda i: (i, 0)),
                  pl.BlockSpec((8, d), lambda i: (0, gate_idx)),
                  pl.BlockSpec((8, d), lambda i: (0, shift_idx)),
                  pl.BlockSpec((8, d), lambda i: (0, scale_idx)),
                  pl.BlockSpec((1, d), lambda i: (0, 0))],
        out_specs=[pl.BlockSpec((tm, d), lambda i: (i, 0)),
                   pl.BlockSpec((tm, d), lambda i: (i, 0))],
        out_shape=[jax.ShapeDtypeStruct((t, d), F32), jax.ShapeDtypeStruct((t, d), BF16)],
        compiler_params=_cparams("parallel"),
        name="out_proj_norm",
    )(a, w, x, ada, ada, ada, gain.reshape(1, d))


def _nt_proj_call(h, w_qt, w_vt, w_gt, q_scale):
    t, d = h.shape
    tq = SLC_CHUNK
    nqr = w_qt.shape[0]
    nv = w_vt.shape[0] // 2
    ng = w_gt.shape[0]
    per = tq // WIN_CHUNK

    def body(h_ref, wq_ref, wv_ref, wg_ref, q_ref, vs_ref, vw_ref, g_ref):
        hv = h_ref[...]
        q_ref[...] = (_nt_dot(wq_ref[...], hv) * q_scale).astype(BF16)
        r = _nt_dot(wv_ref[...], hv)
        vs_ref[0] = r[0:nv].astype(BF16)
        for kk in range(per):
            vw_ref[kk] = r[nv:2 * nv, kk * WIN_CHUNK:(kk + 1) * WIN_CHUNK].astype(BF16)
        g_ref[...] = _sigmoid(_nt_dot(wg_ref[...], hv))

    return pl.pallas_call(
        body,
        grid=(t // tq,),
        in_specs=[pl.BlockSpec((tq, d), lambda i: (i, 0)),
                  pl.BlockSpec((nqr, d), lambda i: (0, 0)),
                  pl.BlockSpec((2 * nv, d), lambda i: (0, 0)),
                  pl.BlockSpec((ng, d), lambda i: (0, 0))],
        out_specs=[pl.BlockSpec((nqr, tq), lambda i: (0, i)),
                   pl.BlockSpec((1, nv, tq), lambda i: (i, 0, 0)),
                   pl.BlockSpec((per, nv, WIN_CHUNK), lambda i: (i, 0, 0)),
                   pl.BlockSpec((ng, tq), lambda i: (0, i))],
        out_shape=[jax.ShapeDtypeStruct((nqr, t), BF16),
                   jax.ShapeDtypeStruct((t // tq, nv, tq), BF16),
                   jax.ShapeDtypeStruct((t // WIN_CHUNK, nv, WIN_CHUNK), BF16),
                   jax.ShapeDtypeStruct((ng, t), F32)],
        compiler_params=_cparams("parallel"),
        name="nt_proj",
    )(h, w_qt, w_vt, w_gt)


def _compress_call(xc, wa, wb, pe8, w2, *, transposed, cc):
    nc, kk = xc.shape

    def body(x_ref, wa_ref, wb_ref, pe_ref, w2_ref, o_ref):
        xv = x_ref[...]
        top = jnp.dot(xv, wa_ref[...], preferred_element_type=F32)
        bot = jnp.dot(xv, wb_ref[...], preferred_element_type=F32)
        pe_top = jnp.dot(pe_ref[...], wa_ref[...], preferred_element_type=F32)
        pe_bot = jnp.dot(pe_ref[...], wb_ref[...], preferred_element_type=F32)
        pre = top + pltpu.roll(bot, nc - 1, 0) + pe_top[0:1] + pe_bot[1:2]
        hid = (pre * _sigmoid(pre)).astype(BF16)
        if transposed:
            r = _nt_dot(w2_ref[...], hid)
            for c in range(nc // cc):
                o_ref[c] = r[:, c * cc:(c + 1) * cc].astype(BF16)
        else:
            o_ref[...] = jnp.dot(hid, w2_ref[...], preferred_element_type=F32).astype(BF16)

    if transposed:
        out_spec = pl.BlockSpec((None, nc // cc, HEAD_DIM, cc), lambda g: (g, 0, 0, 0))
        out_shape = jax.ShapeDtypeStruct((N_GROUPS, nc // cc, HEAD_DIM, cc), BF16)
    else:
        out_spec = pl.BlockSpec((None, nc, HEAD_DIM), lambda g: (g, 0, 0))
        out_shape = jax.ShapeDtypeStruct((N_GROUPS, nc, HEAD_DIM), BF16)
    return pl.pallas_call(
        body,
        grid=(N_GROUPS,),
        in_specs=[pl.BlockSpec((nc, kk), lambda g: (0, 0)),
                  pl.BlockSpec((None, kk, CMP_HIDDEN), lambda g: (g, 0, 0)),
                  pl.BlockSpec((None, kk, CMP_HIDDEN), lambda g: (g, 0, 0)),
                  pl.BlockSpec((8, kk), lambda g: (0, 0)),
                  pl.BlockSpec(w2.shape, lambda g: (0, 0))],
        out_specs=out_spec,
        out_shape=out_shape,
        compiler_params=_cparams("parallel"),
        name="compress_v" if transposed else "compress_k",
    )(xc, wa, wb, pe8, w2)


def _nsa_feature_tables(nc):
    def table(nrows, pos_step, coarse, onehots):
        n = np.arange(nrows, dtype=np.int32)
        pos = n * pos_step
        hi = (pos // coarse) * coarse
        lo = pos - hi
        cols = [hi, hi, hi, lo, lo, lo] + [np.zeros_like(n)] * (FEAT_ONEHOT - 6)
        if onehots:
            cols += [(n // SEL_BLOCK == b).astype(np.int32) for b in range(SLC_CHUNK // SEL_BLOCK)]
        cols += [np.zeros_like(n)] * (HEAD_DIM - len(cols))
        return jnp.asarray(np.stack(cols, axis=1).astype(np.float32)).astype(BF16)

    slc = table(SLC_CHUNK, 1, 16, True)
    win = table(WINDOW + Q_TILE, 1, 16, False)
    cmp_ = table(min(CMP_CHUNK, nc), CMP_STRIDE, 256, False)
    return slc, win, cmp_


def _nsa_call(qt, proj, kc, vct, vst, vwt, gates, slopes, qfeat, *, t, ks_col, kw_col):
    nq = t // Q_TILE
    n_sel = t // SEL_BLOCK
    nc = t // CMP_STRIDE
    cc = min(CMP_CHUNK, nc)
    top_n = min(SEL_TOP_N, n_sel)
    hg = HEADS_PER_GROUP
    wide = hg * Q_TILE
    sel_per_chunk = SLC_CHUNK // SEL_BLOCK
    win_span = WINDOW + Q_TILE
    kdim = 2 * HEAD_DIM
    tab_slc, tab_win, tab_cmp = _nsa_feature_tables(nc)

    def body(q_ref, kc_ref, vct_ref, ks_ref, vst_ref, kw_ref, vwt_ref, g_ref, sl_ref, qf_ref,
             ts_ref, tw_ref, tc_ref, o_ref,
             qt_ref, kss_ref, ksw_ref, ksc_ref, m_ref, l_ref, acc_ref, p_ref, mw_ref, lw_ref, accw_ref, pw_ref,
             out_ref, s_ref, ps_ref, score_ref, sel_ref, flag_ref):
        g = pl.program_id(0)
        i = pl.program_id(1)
        t0 = i * Q_TILE
        tq = t0 + lax.broadcasted_iota(jnp.int32, (1, Q_TILE), 1)

        for h in range(hg):
            qt_ref[0:HEAD_DIM, h * Q_TILE:(h + 1) * Q_TILE] = q_ref[h * HEAD_DIM:(h + 1) * HEAD_DIM, :]
        qt_ref[HEAD_DIM:HEAD_DIM + FEAT_ONEHOT, :] = qf_ref[...]
        qt_ref[HEAD_DIM + FEAT_ONEHOT:kdim, :] = jnp.zeros((HEAD_DIM - FEAT_ONEHOT, wide), BF16)
        kss_ref[:, HEAD_DIM:kdim] = ts_ref[...]
        ksw_ref[:, HEAD_DIM:kdim] = tw_ref[...]
        ksc_ref[:, HEAD_DIM:kdim] = tc_ref[...]

        state_main = (m_ref, l_ref, acc_ref, p_ref)
        state_win = (mw_ref, lw_ref, accw_ref, pw_ref)

        def reset(state):
            state[0][...] = jnp.full((1, wide), NEG_INF, F32)
            state[1][...] = jnp.zeros((1, wide), F32)
            state[2][...] = jnp.zeros((HEAD_DIM, wide), F32)

        def chunk_offset(ref_pos):
            return sl_ref[...] * (ref_pos - t0).astype(F32)

        hh = hg // 2

        def scores(kst_ref, nrows):
            kst = kst_ref[0:nrows, :]
            return [jnp.dot(kst, qt_ref[:, half * hh * Q_TILE:(half + 1) * hh * Q_TILE],
                            preferred_element_type=F32) for half in range(2)]

        def softmax_pv(state, sts, vt_c, off, nrows, mask=None):
            m_ref, l_ref, acc_ref, p_ref = state
            for half in range(2):
                lanes = slice(half * hh * Q_TILE, (half + 1) * hh * Q_TILE)
                st = sts[half]
                alphas = []
                for k in range(hh):
                    sl = slice((half * hh + k) * Q_TILE, (half * hh + k + 1) * Q_TILE)
                    s = st[:, k * Q_TILE:(k + 1) * Q_TILE]
                    if mask is not None:
                        s = jnp.where(mask, s, NEG_INF)
                    o = off[:, sl]
                    m_old = m_ref[:, sl]
                    m_new = jnp.maximum(m_old, jnp.max(s, axis=0, keepdims=True) + o)
                    alpha = jnp.exp2(m_old - m_new)
                    p = jnp.exp2(s - (m_new - o))
                    l_ref[:, sl] = alpha * l_ref[:, sl] + jnp.sum(p, axis=0, keepdims=True)
                    m_ref[:, sl] = m_new
                    alphas.append(alpha)
                    p_ref[0:nrows, sl] = p.astype(BF16)
                pv = jnp.dot(vt_c, p_ref[0:nrows, lanes], preferred_element_type=F32)
                acc_ref[:, lanes] = acc_ref[:, lanes] * jnp.concatenate(alphas, axis=1) + pv

        def attend(state, kst_ref, vt_c, off, nrows, mask=None):
            softmax_pv(state, scores(kst_ref, nrows), vt_c, off, nrows, mask)

        def inv_norm(state, sl):
            valid = state[0][:, sl] > 0.5 * NEG_INF
            return jnp.where(valid, 1.0 / jnp.maximum(state[1][:, sl], 1e-30), 0.0)

        def finish(state, branch, first):
            for h in range(hg):
                sl = slice(h * Q_TILE, (h + 1) * Q_TILE)
                gate = g_ref[pl.ds(branch * N_HEADS + g * hg + h, 1), :]
                contrib = state[2][:, sl] * (inv_norm(state, sl) * gate)
                if first:
                    out_ref[:, sl] = contrib
                else:
                    out_ref[:, sl] += contrib

        reset(state_main)
        ps_ref[...] = jnp.zeros(ps_ref.shape, F32)
        n_cmp_keys = (t0 + Q_TILE - CMP_BLOCK) // CMP_STRIDE + 1
        n_cc = (n_cmp_keys + cc - 1) // cc

        def cmp_scores(c, carry):
            r0 = pl.multiple_of(c * cc, cc)
            n = r0 + lax.broadcasted_iota(jnp.int32, (cc, Q_TILE), 0)
            mask = (n * CMP_STRIDE + (CMP_BLOCK - 1)) <= tq
            off = chunk_offset(r0 * CMP_STRIDE + (CMP_BLOCK - 1))
            ksc_ref[:, 0:HEAD_DIM] = kc_ref[pl.ds(r0, cc), :]
            for hp in range(hg // 2):
                pair = slice(hp * 2 * Q_TILE, (hp + 1) * 2 * Q_TILE)
                st = jnp.dot(ksc_ref[...], qt_ref[:, pair], preferred_element_type=F32)
                for hh in range(2):
                    sl = slice((2 * hp + hh) * Q_TILE, (2 * hp + hh + 1) * Q_TILE)
                    s = jnp.where(mask, st[:, hh * Q_TILE:(hh + 1) * Q_TILE] + off[:, sl], NEG_INF)
                    s_ref[pl.ds(r0, cc), sl] = s
                    m_ref[:, sl] = jnp.maximum(m_ref[:, sl], jnp.max(s, axis=0, keepdims=True))
            return carry

        lax.fori_loop(0, n_cc, cmp_scores, 0)

        def cmp_probs(c, carry):
            r0 = pl.multiple_of(c * cc, cc)
            for hp in range(hg // 2):
                pair = slice(hp * 2 * Q_TILE, (hp + 1) * 2 * Q_TILE)
                for hh in range(2):
                    sl = slice((2 * hp + hh) * Q_TILE, (2 * hp + hh + 1) * Q_TILE)
                    p = jnp.exp2(s_ref[pl.ds(r0, cc), sl] - m_ref[:, sl])
                    l_ref[:, sl] += jnp.sum(p, axis=0, keepdims=True)
                    s_ref[pl.ds(r0, cc), sl] = p
                    p_ref[0:cc, sl] = p.astype(BF16)
                acc_ref[:, pair] += jnp.dot(vct_ref[c], p_ref[0:cc, pair], preferred_element_type=F32)
            return carry

        lax.fori_loop(0, n_cc, cmp_probs, 0)
        finish(state_main, 0, True)

        def ps_body(c, carry):
            r0 = pl.multiple_of(c * cc, cc)
            tot = jnp.zeros((cc, Q_TILE), F32)
            for h in range(hg):
                sl = slice(h * Q_TILE, (h + 1) * Q_TILE)
                tot = tot + s_ref[pl.ds(r0, cc), sl] * inv_norm(state_main, sl)
            ps_ref[pl.ds(8 + r0, cc), :] = tot
            return carry

        lax.fori_loop(0, n_cc, ps_body, 0)

        ratio = SEL_BLOCK // CMP_STRIDE
        imp = ps_ref[pl.ds(7, n_sel, stride=ratio), :]
        for kk in range(ratio):
            imp = imp + ps_ref[pl.ds(8 + kk, n_sel, stride=ratio), :]
        jrow = lax.broadcasted_iota(jnp.int32, (n_sel, Q_TILE), 0)
        cur = tq // SEL_BLOCK
        forced = jnp.where(jrow == 0, 1.0, jnp.where(jrow == cur, 1.0, jnp.where(jrow == cur - 1, 1.0, 0.0)))
        score_ref[...] = jnp.where(forced > 0.5, -2.0, jnp.where(jrow <= cur, imp, -1.0))
        sel_ref[...] = forced
        jrow_f = jrow.astype(F32)

        def topk_body(r, carry):
            sc = score_ref[...]
            mx = jnp.max(sc, axis=0, keepdims=True)
            first = jnp.min(jnp.where(sc == mx, jrow_f, float(n_sel)), axis=0, keepdims=True)
            pick = jrow_f == first
            sel_ref[...] = jnp.where(pick, 1.0, sel_ref[...])
            score_ref[...] = jnp.where(pick, -2.0, sc)
            return carry

        lax.fori_loop(0, top_n - N_FORCED, topk_body, 0)
        for c in range(t // SLC_CHUNK):
            blocks = sel_ref[c * sel_per_chunk:(c + 1) * sel_per_chunk, :]
            flag_ref[c] = (jnp.max(blocks) > 0.5).astype(jnp.int32)

        reset(state_main)
        last_sc = (t0 + Q_TILE - 1) // SLC_CHUNK

        def slc_stage(c):
            r0 = pl.multiple_of(c * SLC_CHUNK, SLC_CHUNK)
            kss_ref[:, 0:HEAD_DIM] = ks_ref[pl.ds(r0, SLC_CHUNK), :]
            sel8 = sel_ref[pl.ds(pl.multiple_of(c * sel_per_chunk, sel_per_chunk), sel_per_chunk), :]
            bias = (sel8 - 1.0) * MASK_BIG
            rows = jnp.concatenate([jnp.concatenate([bias] * hg, axis=1),
                                    jnp.zeros((FEAT_ONEHOT - sel_per_chunk, wide), F32)], axis=0)
            qt_ref[HEAD_DIM + FEAT_ONEHOT:HEAD_DIM + 2 * FEAT_ONEHOT, :] = rows.astype(BF16)
            return r0

        def slc_body(c, carry):
            @pl.when(flag_ref[c] > 0)
            def _():
                r0 = slc_stage(c)
                attend(state_main, kss_ref, vst_ref[c], chunk_offset(r0), SLC_CHUNK)

            return carry

        lax.fori_loop(0, last_sc, slc_body, 0)

        reset(state_win)
        r0s = slc_stage(last_sc)
        n_wc = win_span // WIN_CHUNK
        c0 = jnp.maximum(i * (Q_TILE // WIN_CHUNK) - WINDOW // WIN_CHUNK, 0)
        r0w = pl.multiple_of(c0 * WIN_CHUNK, WIN_CHUNK)
        ksw_ref[:, 0:HEAD_DIM] = kw_ref[pl.ds(r0w, win_span), :]
        sts_slc = scores(kss_ref, SLC_CHUNK)
        sts_win = scores(ksw_ref, win_span)

        masks = (r0s + lax.broadcasted_iota(jnp.int32, (SLC_CHUNK, Q_TILE), 0)) <= tq
        softmax_pv(state_main, sts_slc, vst_ref[last_sc], chunk_offset(r0s), SLC_CHUNK, masks)
        finish(state_main, 1, False)

        dposw = (r0w + lax.broadcasted_iota(jnp.int32, (win_span, Q_TILE), 0)) - tq
        maskw = jnp.where(dposw <= 0, dposw, -2 * WINDOW) > -WINDOW
        vtw = jnp.concatenate([vwt_ref[c0 + kk] for kk in range(n_wc)], axis=1)
        softmax_pv(state_win, sts_win, vtw, chunk_offset(r0w), win_span, maskw)
        finish(state_win, 2, False)

        for h in range(hg):
            sl = slice(h * Q_TILE, (h + 1) * Q_TILE)
            o_ref[:, h * HEAD_DIM:(h + 1) * HEAD_DIM] = out_ref[:, sl].T.astype(o_ref.dtype)

    one = pl.Buffered(1)
    in_specs = [
        pl.BlockSpec((hg * HEAD_DIM, Q_TILE), lambda g, i: (g, i)),
        pl.BlockSpec((None, nc, HEAD_DIM), lambda g, i: (g, 0, 0)),
        pl.BlockSpec((None, nc // cc, HEAD_DIM, cc), lambda g, i: (g, 0, 0, 0)),
        pl.BlockSpec((t, HEAD_DIM), lambda g, i: (0, ks_col + g), pipeline_mode=one),
        pl.BlockSpec((t // SLC_CHUNK, HEAD_DIM, SLC_CHUNK), lambda g, i: (0, g, 0), pipeline_mode=one),
        pl.BlockSpec((t, HEAD_DIM), lambda g, i: (0, kw_col + g), pipeline_mode=one),
        pl.BlockSpec((t // WIN_CHUNK, HEAD_DIM, WIN_CHUNK), lambda g, i: (0, g, 0), pipeline_mode=one),
        pl.BlockSpec((gates.shape[0], Q_TILE), lambda g, i: (0, i)),
        pl.BlockSpec((None, 1, wide), lambda g, i: (g, 0, 0)),
        pl.BlockSpec((None, FEAT_ONEHOT, wide), lambda g, i: (g, 0, 0)),
        pl.BlockSpec(tab_slc.shape, lambda g, i: (0, 0)),
        pl.BlockSpec(tab_win.shape, lambda g, i: (0, 0)),
        pl.BlockSpec(tab_cmp.shape, lambda g, i: (0, 0)),
    ]
    scratch = [
        pltpu.VMEM((kdim, wide), BF16),
        pltpu.VMEM((SLC_CHUNK, kdim), BF16),
        pltpu.VMEM((win_span, kdim), BF16),
        pltpu.VMEM((cc, kdim), BF16),
        pltpu.VMEM((1, wide), F32),
        pltpu.VMEM((1, wide), F32),
        pltpu.VMEM((HEAD_DIM, wide), F32),
        pltpu.VMEM((SLC_CHUNK, wide), BF16),
        pltpu.VMEM((1, wide), F32),
        pltpu.VMEM((1, wide), F32),
        pltpu.VMEM((HEAD_DIM, wide), F32),
        pltpu.VMEM((win_span, wide), BF16),
        pltpu.VMEM((HEAD_DIM, wide), F32),
        pltpu.VMEM((nc, wide), F32),
        pltpu.VMEM((8 + nc, Q_TILE), F32),
        pltpu.VMEM((n_sel, Q_TILE), F32),
        pltpu.VMEM((n_sel, Q_TILE), F32),
        pltpu.SMEM((t // SLC_CHUNK,), jnp.int32),
    ]
    return pl.pallas_call(
        body,
        grid=(N_GROUPS, nq),
        in_specs=in_specs,
        out_specs=pl.BlockSpec((Q_TILE, hg * HEAD_DIM), lambda g, i: (i, g)),
        out_shape=jax.ShapeDtypeStruct((t, N_HEADS * HEAD_DIM), BF16),
        scratch_shapes=scratch,
        compiler_params=_cparams("parallel", "parallel"),
        name="nsa",
    )(qt, kc, vct, proj, vst, proj, vwt, gates, slopes, qfeat, tab_slc, tab_win, tab_cmp)


def _glu_proj_call(h, w_glu):
    t, d = h.shape
    ch = w_glu.shape[1] // 2
    tm = min(1024, t)
    tn = 512
    nj = ch // tn

    def body(h_ref, wa_ref, wg_ref, o_ref):
        hv = h_ref[...]
        a = jnp.dot(hv, wa_ref[...], preferred_element_type=F32)
        g = jnp.dot(hv, wg_ref[...], preferred_element_type=F32)
        o_ref[...] = (a * _sigmoid(g)).astype(o_ref.dtype)

    return pl.pallas_call(
        body,
        grid=(t // tm, nj),
        in_specs=[pl.BlockSpec((tm, d), lambda i, j: (i, 0)),
                  pl.BlockSpec((d, tn), lambda i, j: (0, j)),
                  pl.BlockSpec((d, tn), lambda i, j: (0, nj + j))],
        out_specs=pl.BlockSpec((tm, tn), lambda i, j: (i, j)),
        out_shape=jax.ShapeDtypeStruct((t, ch), BF16),
        compiler_params=_cparams("parallel", "parallel"),
        name="proj_glu",
    )(h, w_glu, w_glu)


def _conv_front_call(u, w_dw, b_dw, ln_g, ln_b):
    t = u.shape[0]
    ch = w_dw.shape[1]
    tm = min(256, t)
    halo = 32
    rb = 32
    strip = 512
    w_rep = jnp.repeat(w_dw, F32_SUBLANES, axis=0)

    def body(a_ref, ah_ref, w_ref, b_ref, lg_ref, lb_ref, o_ref, u_ref, sb_ref, y_ref):
        i = pl.program_id(0)
        u_ref[halo:halo + tm, :] = a_ref[...].astype(F32)
        u_ref[0:halo, :] = jnp.where(i > 0, ah_ref[...].astype(F32), 0.0)
        for cs in range(ch // strip):
            cols = slice(cs * strip, (cs + 1) * strip)
            ue = u_ref[:, cols]
            sb_ref[0] = ue
            for part in range(1, F32_SUBLANES):
                sb_ref[part] = pltpu.roll(ue, part, 0)

            def rows(r, carry, cols=cols):
                r0 = pl.multiple_of(r * rb, rb)
                acc = jnp.zeros((rb // F32_SUBLANES, F32_SUBLANES, strip), F32)
                for s in range(CONV_WIDTH):
                    whole, part = divmod(s, F32_SUBLANES)
                    k = CONV_WIDTH - 1 - s
                    tap = sb_ref[part, pl.ds(halo + r0 - F32_SUBLANES * whole, rb), :]
                    wk = w_ref[k * F32_SUBLANES:(k + 1) * F32_SUBLANES, cols]
                    acc = acc + tap.reshape(rb // F32_SUBLANES, F32_SUBLANES, strip) * wk
                y_ref[pl.ds(r0, rb), cols] = acc.reshape(rb, strip) + b_ref[:, cols]
                return carry

            lax.fori_loop(0, tm // rb, rows, 0)
        y = y_ref[...]
        mu = jnp.mean(y, axis=-1, keepdims=True)
        yc = y - mu
        var = jnp.mean(yc * yc, axis=-1, keepdims=True)
        z = yc * lax.rsqrt(var + EPS) * lg_ref[...] + lb_ref[...]
        o_ref[...] = (z * _sigmoid(z)).astype(o_ref.dtype)

    per = tm // halo
    return pl.pallas_call(
        body,
        grid=(t // tm,),
        in_specs=[pl.BlockSpec((tm, ch), lambda i: (i, 0)),
                  pl.BlockSpec((halo, ch), lambda i: (jnp.maximum(i * per - 1, 0), 0)),
                  pl.BlockSpec((CONV_WIDTH * F32_SUBLANES, ch), lambda i: (0, 0)),
                  pl.BlockSpec((1, ch), lambda i: (0, 0)),
                  pl.BlockSpec((1, ch), lambda i: (0, 0)),
                  pl.BlockSpec((1, ch), lambda i: (0, 0))],
        out_specs=pl.BlockSpec((tm, ch), lambda i: (i, 0)),
        out_shape=jax.ShapeDtypeStruct((t, ch), BF16),
        scratch_shapes=[pltpu.VMEM((halo + tm, ch), F32),
                        pltpu.VMEM((F32_SUBLANES, halo + tm, strip), F32),
                        pltpu.VMEM((tm, ch), F32)],
        compiler_params=_cparams("parallel"),
        name="conv_front",
    )(u, u, w_rep, b_dw.reshape(1, ch), ln_g.reshape(1, ch), ln_b.reshape(1, ch))


def _ffn_up_call(h2, w_up, w_dw, b_dw):
    t, d = h2.shape
    n2 = w_up.shape[1]
    f = n2 // 2
    tm = min(1024, t)
    rsub = 256
    tn = 512
    nj = f // tn
    sub = F32_SUBLANES
    w_pad = jnp.concatenate([w_dw, jnp.zeros((sub - FFN_CONV_WIDTH, n2), F32)], axis=0)
    b2 = b_dw.reshape(1, n2)

    def body(h_ref, wa_ref, wg_ref, da_ref, dg_ref, ba_ref, bg_ref, o_ref, ca_ref, cg_ref):
        i = pl.program_id(0)
        j = pl.program_id(1)

        @pl.when(i == 0)
        def _():
            ca_ref[j] = jnp.zeros((sub, tn), F32)
            cg_ref[j] = jnp.zeros((sub, tn), F32)

        wa = wa_ref[...]
        wg = wg_ref[...]

        def conv(up, prev, d_ref, b_ref):
            ext = jnp.concatenate([prev, up], axis=0)
            acc = b_ref[...] + up * d_ref[FFN_CONV_WIDTH - 1:FFN_CONV_WIDTH, :]
            for s in range(1, FFN_CONV_WIDTH):
                k = FFN_CONV_WIDTH - 1 - s
                acc = acc + pltpu.roll(ext, s, 0)[sub:, :] * d_ref[k:k + 1, :]
            return acc

        prev_a = ca_ref[j]
        prev_g = cg_ref[j]
        for rs in range(tm // rsub):
            rows = slice(rs * rsub, (rs + 1) * rsub)
            hv = h_ref[rows, :]
            up_a = jnp.dot(hv, wa, preferred_element_type=F32)
            up_g = jnp.dot(hv, wg, preferred_element_type=F32)
            ca = conv(up_a, prev_a, da_ref, ba_ref)
            cg = conv(up_g, prev_g, dg_ref, bg_ref)
            o_ref[rows, :] = (cg * _sigmoid(cg) * ca).astype(o_ref.dtype)
            prev_a = up_a[rsub - sub:, :]
            prev_g = up_g[rsub - sub:, :]
        ca_ref[j] = prev_a
        cg_ref[j] = prev_g

    return pl.pallas_call(
        body,
        grid=(t // tm, nj),
        in_specs=[pl.BlockSpec((tm, d), lambda i, j: (i, 0)),
                  pl.BlockSpec((d, tn), lambda i, j: (0, j)),
                  pl.BlockSpec((d, tn), lambda i, j: (0, nj + j)),
                  pl.BlockSpec((sub, tn), lambda i, j: (0, j)),
                  pl.BlockSpec((sub, tn), lambda i, j: (0, nj + j)),
                  pl.BlockSpec((1, tn), lambda i, j: (0, j)),
                  pl.BlockSpec((1, tn), lambda i, j: (0, nj + j))],
        out_specs=pl.BlockSpec((tm, tn), lambda i, j: (i, j)),
        out_shape=jax.ShapeDtypeStruct((t, f), BF16),
        scratch_shapes=[pltpu.VMEM((nj, sub, tn), F32), pltpu.VMEM((nj, sub, tn), F32)],
        compiler_params=_cparams("arbitrary", "arbitrary"),
        name="ffn_up",
    )(h2, w_up, w_up, w_pad, w_pad, b2, b2)


def _layer(x, c_row, p):
    t, d = x.shape
    ada = _ada_call(c_row, p["w_ada"], p["b_ada"])

    w_in = p["w_in"]
    qw = N_HEADS * HEAD_DIM
    kvw = N_GROUPS * HEAD_DIM
    o = [0, qw]
    for _ in range(6):
        o.append(o[-1] + kvw)
    o.append(o[-1] + 3 * N_HEADS)
    o.append(o[-1] + 2 * d)
    o.append(o[-1] + 2 * d)
    w_q, w_kc, w_vc, w_ks, w_vs, w_kw, w_vw, w_gn, w_glu, w_mg = [w_in[:, o[k]:o[k + 1]] for k in range(10)]
    w_rm = jnp.concatenate([w_kc, w_vc, w_ks, w_kw], axis=1).astype(BF16)
    w_glu = w_glu.astype(BF16)
    w_mg = w_mg.astype(BF16)
    w_qt = w_q.T.astype(BF16)
    w_vt = jnp.concatenate([w_vs, w_vw], axis=1).T.astype(BF16)
    n_gate_rows = 64
    w_gt = jnp.concatenate([w_gn.T, jnp.zeros((n_gate_rows - 3 * N_HEADS, d), F32)], axis=0).astype(BF16)

    h1 = _norm_call(x, p["norm1_g"], ada, shift_idx=0, scale_idx=1)
    proj = _matmul_call(h1, w_rm, tm=1024, tn=4 * kvw, out_dtype=BF16, name="proj_rm")
    u = _glu_proj_call(h1, w_glu)
    gm = _matmul_call(h1, w_mg, tm=1024, tn=1024, out_dtype=BF16, name="proj_mg")
    qt, vst, vwt, gates = _nt_proj_call(h1, w_qt, w_vt, w_gt, HEAD_DIM ** -0.5 * LOG2E)

    nc = t // CMP_STRIDE
    cc = min(CMP_CHUNK, nc)
    half = CMP_BLOCK // CMP_STRIDE
    assert half == 2

    def expand(w1):
        w1r = w1.reshape(half, CMP_STRIDE, HEAD_DIM, CMP_HIDDEN)
        outs = []
        for hf in range(half):
            per_g = []
            for g in range(N_GROUPS):
                z = jnp.zeros((CMP_STRIDE, N_GROUPS, HEAD_DIM, CMP_HIDDEN), F32).at[:, g].set(w1r[hf])
                per_g.append(z.reshape(CMP_STRIDE * kvw, CMP_HIDDEN))
            outs.append(jnp.stack(per_g).astype(BF16))
        return outs

    pe = p["cmp_pe"].reshape(half, CMP_STRIDE, 1, HEAD_DIM)
    pe_rows = jnp.broadcast_to(pe, (half, CMP_STRIDE, N_GROUPS, HEAD_DIM)).reshape(half, CMP_STRIDE * kvw)
    pe8 = jnp.concatenate([pe_rows, jnp.zeros((8 - half, CMP_STRIDE * kvw), F32)], axis=0).astype(BF16)
    kc_raw = proj[:, 0:kvw].reshape(nc, CMP_STRIDE * kvw)
    vc_raw = proj[:, kvw:2 * kvw].reshape(nc, CMP_STRIDE * kvw)
    wka, wkb = expand(p["w_kc1"])
    wva, wvb = expand(p["w_vc1"])
    kc = _compress_call(kc_raw, wka, wkb, pe8, p["w_kc2"].astype(BF16), transposed=False, cc=cc)
    vct = _compress_call(vc_raw, wva, wvb, pe8, p["w_vc2"].T.astype(BF16), transposed=True, cc=cc)

    head_ids = np.arange(1, N_HEADS + 1, dtype=np.float64)
    slopes_np = (np.exp2(-8.0 * head_ids / N_HEADS) * LOG2E).astype(np.float32)
    slopes_np = np.broadcast_to(slopes_np.reshape(N_GROUPS, 1, HEADS_PER_GROUP, 1),
                                (N_GROUPS, 1, HEADS_PER_GROUP, Q_TILE)).reshape(N_GROUPS, 1, HEADS_PER_GROUP * Q_TILE)
    s1 = slopes_np.astype(BF16)
    s2 = (slopes_np - s1.astype(np.float32)).astype(BF16)
    s3 = (slopes_np - s1.astype(np.float32) - s2.astype(np.float32)).astype(BF16)
    zero_rows = np.zeros((N_GROUPS, FEAT_ONEHOT - 6, slopes_np.shape[-1]), dtype=s1.dtype)
    qfeat = jnp.asarray(np.concatenate([s1, s2, s3, s1, s2, s3, zero_rows], axis=1))
    slopes = jnp.asarray(slopes_np)
    o_nsa = _nsa_call(qt, proj, kc, vct, vst, vwt, gates, slopes, qfeat, t=t,
                      ks_col=(2 * kvw) // HEAD_DIM, kw_col=(3 * kvw) // HEAD_DIM)

    u2 = _conv_front_call(u, p["conv_dw_w"], p["conv_dw_b"], p["conv_ln_g"], p["conv_ln_b"])

    tn = 1024
    nb = d // tn
    ya = _matmul_call(o_nsa, p["w_o_nsa"].astype(BF16), tm=512, tn=tn, out_dtype=F32, name="o_proj",
                      epilogue=lambda acc, ga_ref: acc * _sigmoid(ga_ref[...].astype(F32)),
                      extras=(gm,), extra_specs=(pl.BlockSpec((512, tn), lambda i, j: (i, j)),))
    merged = _matmul_call(
        u2, p["conv_pw_w"].astype(BF16), tm=512, tn=tn, out_dtype=BF16, name="pw_merge",
        epilogue=lambda acc, b_ref, gb_ref, ya_ref: (acc + b_ref[...]) * _sigmoid(gb_ref[...].astype(F32)) + ya_ref[...],
        extras=(p["conv_pw_b"].reshape(1, d), gm, ya),
        extra_specs=(pl.BlockSpec((1, tn), lambda i, j: (0, j)),
                     pl.BlockSpec((512, tn), lambda i, j: (i, nb + j)),
                     pl.BlockSpec((512, tn), lambda i, j: (i, j))))
    x1, h2 = _out_norm_call(merged, p["w_out"].astype(BF16), x, ada, p["norm2_g"],
                            gate_idx=2, shift_idx=3, scale_idx=4)
    act = _ffn_up_call(h2, p["ffn_w_up"].astype(BF16), p["ffn_dw_w"], p["ffn_dw_b"])
    x2 = _matmul_call(
        act, p["ffn_w_down"].astype(BF16), tm=512, tn=512, out_dtype=F32, name="ffn_down",
        epilogue=lambda acc, x_ref, g_ref: x_ref[...] + g_ref[0:1, :] * acc,
        extras=(x1, ada),
        extra_specs=(pl.BlockSpec((512, 512), lambda i, j: (i, j)),
                     pl.BlockSpec((8, 512), lambda i, j: (0, 5 * (d // 512) + j))))
    return x2


def kernel(x, c, w_ada, b_ada, norm1_g, w_in, cmp_pe, w_kc1, w_kc2, w_vc1, w_vc2, w_o_nsa, conv_dw_w, conv_dw_b, conv_ln_g, conv_ln_b, conv_pw_w, conv_pw_b, w_out, norm2_g, ffn_w_up, ffn_dw_w, ffn_dw_b, ffn_w_down, final_g):
    b, t, d = x.shape
    stacked = dict(w_ada=w_ada, b_ada=b_ada, norm1_g=norm1_g, w_in=w_in, cmp_pe=cmp_pe, w_kc1=w_kc1,
                   w_kc2=w_kc2, w_vc1=w_vc1, w_vc2=w_vc2, w_o_nsa=w_o_nsa, conv_dw_w=conv_dw_w,
                   conv_dw_b=conv_dw_b, conv_ln_g=conv_ln_g, conv_ln_b=conv_ln_b, conv_pw_w=conv_pw_w,
                   conv_pw_b=conv_pw_b, w_out=w_out, norm2_g=norm2_g, ffn_w_up=ffn_w_up,
                   ffn_dw_w=ffn_dw_w, ffn_dw_b=ffn_dw_b, ffn_w_down=ffn_w_down)
    depth = w_ada.shape[0]
    outs = []
    for bi in range(b):
        xb = x[bi]
        for layer in range(depth):
            xb = _layer(xb, c[bi:bi + 1], {k: v[layer] for k, v in stacked.items()})
        outs.append(_norm_call(xb, final_g, out_dtype=x.dtype))
    return jnp.stack(outs)
```

```python
import functools

import jax
import jax.numpy as jnp
import numpy as np
from jax import lax
from jax.experimental import pallas as pl
from jax.experimental.pallas import tpu as pltpu

F32 = jnp.float32
BF16 = jnp.bfloat16

N_HEADS = 16
N_GROUPS = 2
HEADS_PER_GROUP = N_HEADS // N_GROUPS
HEAD_DIM = 128
CMP_BLOCK = 32
CMP_STRIDE = 16
CMP_HIDDEN = 2 * HEAD_DIM
SEL_BLOCK = 64
SEL_TOP_N = 16
WINDOW = 512
FORCE_BONUS = 1e6
CONV_WIDTH = 31
FFN_CONV_WIDTH = 3
EPS = 1e-6
NEG_INF = -1e30
LOG2E = 1.4426950408889634

V7X_VMEM_BYTES = 64 * 1024 * 1024
VMEM_LIMIT_BYTES = V7X_VMEM_BYTES - 8 * 1024 * 1024
LANES = 128
F32_SUBLANES = 8

Q_TILE = 128
SLC_CHUNK = 512
WIN_CHUNK = 128
CMP_CHUNK = 256
FEAT_ONEHOT = 16
MASK_BIG = 2.0 ** 20
N_FORCED = 3
ROW_BLOCK = 256
V_ROWS = HEAD_DIM + 16


def _cparams(*sem):
    return pltpu.CompilerParams(dimension_semantics=sem, vmem_limit_bytes=VMEM_LIMIT_BYTES)


def _sigmoid(v):
    return 1.0 / (1.0 + jnp.exp(-v))


def _nt_dot(a, b):
    return lax.dot_general(a, b, (((1,), (1,)), ((), ())), preferred_element_type=F32)


def _ada_call(c_row, w, b):
    d, n = w.shape
    tn = 1024
    cb = jnp.broadcast_to(c_row.reshape(d, 1), (d, LANES))

    def body(cb_ref, w_ref, b_ref, o_ref):
        cv = cb_ref[...]
        act = cv * _sigmoid(cv)
        for j in range(tn // LANES):
            sl = slice(j * LANES, (j + 1) * LANES)
            row = jnp.sum(w_ref[:, sl] * act, axis=0, keepdims=True) + b_ref[:, sl]
            o_ref[:, sl] = jnp.broadcast_to(row, (8, LANES))

    return pl.pallas_call(
        body,
        grid=(n // tn,),
        in_specs=[pl.BlockSpec((d, LANES), lambda j: (0, 0)),
                  pl.BlockSpec((d, tn), lambda j: (0, j)),
                  pl.BlockSpec((1, tn), lambda j: (0, j))],
        out_specs=pl.BlockSpec((8, tn), lambda j: (0, j)),
        out_shape=jax.ShapeDtypeStruct((8, n), F32),
        compiler_params=_cparams("parallel"),
        name="ada",
    )(cb, w, b.reshape(1, n))


def _norm_call(x, gain, ada=None, shift_idx=0, scale_idx=0, out_dtype=BF16):
    t, d = x.shape
    tm = 512

    def body(*refs):
        if ada is None:
            x_ref, g_ref, o_ref = refs
        else:
            x_ref, g_ref, sh_ref, sc_ref, o_ref = refs
        xv = x_ref[...]
        y = xv * lax.rsqrt(jnp.mean(xv * xv, axis=-1, keepdims=True) + EPS) * g_ref[...]
        if ada is not None:
            y = y * (1.0 + sc_ref[0:1, :]) + sh_ref[0:1, :]
        o_ref[...] = y.astype(o_ref.dtype)

    in_specs = [pl.BlockSpec((tm, d), lambda i: (i, 0)), pl.BlockSpec((1, d), lambda i: (0, 0))]
    args = [x, gain.reshape(1, d)]
    if ada is not None:
        in_specs += [pl.BlockSpec((8, d), lambda i: (0, shift_idx)),
                     pl.BlockSpec((8, d), lambda i: (0, scale_idx))]
        args += [ada, ada]
    return pl.pallas_call(
        body,
        grid=(t // tm,),
        in_specs=in_specs,
        out_specs=pl.BlockSpec((tm, d), lambda i: (i, 0)),
        out_shape=jax.ShapeDtypeStruct((t, d), out_dtype),
        compiler_params=_cparams("parallel"),
        name="norm",
    )(*args)


def _matmul_call(a, w, *, tm, tn, out_dtype, name, epilogue=None, extras=(), extra_specs=()):
    m, k = a.shape
    n = w.shape[1]
    tm = min(tm, m)

    def body(a_ref, w_ref, *rest):
        o_ref = rest[-1]
        acc = jnp.dot(a_ref[...], w_ref[...].astype(BF16), preferred_element_type=F32)
        if epilogue is not None:
            acc = epilogue(acc, *rest[:-1])
        o_ref[...] = acc.astype(o_ref.dtype)

    return pl.pallas_call(
        body,
        grid=(m // tm, n // tn),
        in_specs=[pl.BlockSpec((tm, k), lambda i, j: (i, 0)),
                  pl.BlockSpec((k, tn), lambda i, j: (0, j))] + list(extra_specs),
        out_specs=pl.BlockSpec((tm, tn), lambda i, j: (i, j)),
        out_shape=jax.ShapeDtypeStruct((m, n), out_dtype),
        compiler_params=_cparams("parallel", "parallel"),
        name=name,
    )(a, w, *extras)


def _out_norm_call(a, w, x, ada, gain, *, gate_idx, shift_idx, scale_idx):
    t, d = x.shape
    tm = min(512, t)

    def body(a_ref, w_ref, x_ref, g_ref, sh_ref, sc_ref, n_ref, x1_ref, h_ref):
        acc = jnp.dot(a_ref[...], w_ref[...], preferred_element_type=F32)
        x1 = x_ref[...] + g_ref[0:1, :] * acc
        x1_ref[...] = x1
        y = x1 * lax.rsqrt(jnp.mean(x1 * x1, axis=-1, keepdims=True) + EPS) * n_ref[...]
        h_ref[...] = (y * (1.0 + sc_ref[0:1, :]) + sh_ref[0:1, :]).astype(h_ref.dtype)

    return pl.pallas_call(
        body,
        grid=(t // tm,),
        in_specs=[pl.BlockSpec((tm, a.shape[1]), lambda i: (i, 0)),
                  pl.BlockSpec(w.shape, lambda i: (0, 0), pipeline_mode=pl.Buffered(1)),
                  pl.BlockSpec((tm, d), lambda i: (i, 0)),
                  pl.BlockSpec((8, d), lambda i: (0, gate_idx)),
                  pl.BlockSpec((8, d), lambda i: (0, shift_idx)),
                  pl.BlockSpec((8, d), lambda i: (0, scale_idx)),
                  pl.BlockSpec((1, d), lambda i: (0, 0))],
        out_specs=[pl.BlockSpec((tm, d), lambda i: (i, 0)),
                   pl.BlockSpec((tm, d), lambda i: (i, 0))],
        out_shape=[jax.ShapeDtypeStruct((t, d), F32), jax.ShapeDtypeStruct((t, d), BF16)],
        compiler_params=_cparams("parallel"),
        name="out_proj_norm",
    )(a, w, x, ada, ada, ada, gain.reshape(1, d))


def _nt_proj_call(h, w_qt, w_vt, w_gt, q_scale):
    t, d = h.shape
    tq = SLC_CHUNK
    nqr = w_qt.shape[0]
    nv = w_vt.shape[0] // 2
    ng = w_gt.shape[0]
    per = tq // WIN_CHUNK

    def body(h_ref, wq_ref, wv_ref, wg_ref, q_ref, vs_ref, vw_ref, g_ref):
        hv = h_ref[...]
        q_ref[...] = (_nt_dot(wq_ref[...], hv) * q_scale).astype(BF16)
        r = _nt_dot(wv_ref[...], hv)
        ones_rows = jnp.where(lax.broadcasted_iota(jnp.int32, (V_ROWS - HEAD_DIM, tq), 0) == 0, 1.0, 0.0)
        for g in range(N_GROUPS):
            full_s = jnp.concatenate([r[g * HEAD_DIM:(g + 1) * HEAD_DIM], ones_rows], axis=0).astype(BF16)
            vs_ref[0, g * V_ROWS:(g + 1) * V_ROWS, :] = full_s
            full_w = jnp.concatenate([r[nv + g * HEAD_DIM:nv + (g + 1) * HEAD_DIM], ones_rows], axis=0).astype(BF16)
            for kk in range(per):
                vw_ref[kk, g * V_ROWS:(g + 1) * V_ROWS, :] = full_w[:, kk * WIN_CHUNK:(kk + 1) * WIN_CHUNK]
        g_ref[...] = _sigmoid(_nt_dot(wg_ref[...], hv))

    return pl.pallas_call(
        body,
        grid=(t // tq,),
        in_specs=[pl.BlockSpec((tq, d), lambda i: (i, 0)),
                  pl.BlockSpec((nqr, d), lambda i: (0, 0)),
                  pl.BlockSpec((2 * nv, d), lambda i: (0, 0)),
                  pl.BlockSpec((ng, d), lambda i: (0, 0))],
        out_specs=[pl.BlockSpec((nqr, tq), lambda i: (0, i)),
                   pl.BlockSpec((1, N_GROUPS * V_ROWS, tq), lambda i: (i, 0, 0)),
                   pl.BlockSpec((per, N_GROUPS * V_ROWS, WIN_CHUNK), lambda i: (i, 0, 0)),
                   pl.BlockSpec((ng, tq), lambda i: (0, i))],
        out_shape=[jax.ShapeDtypeStruct((nqr, t), BF16),
                   jax.ShapeDtypeStruct((t // tq, N_GROUPS * V_ROWS, tq), BF16),
                   jax.ShapeDtypeStruct((t // WIN_CHUNK, N_GROUPS * V_ROWS, WIN_CHUNK), BF16),
                   jax.ShapeDtypeStruct((ng, t), F32)],
        compiler_params=_cparams("parallel"),
        name="nt_proj",
    )(h, w_qt, w_vt, w_gt)


def _compress_call(xc, wa, wb, pe8, w2, *, transposed, cc):
    nc, kk = xc.shape

    def body(x_ref, wa_ref, wb_ref, pe_ref, w2_ref, o_ref):
        xv = x_ref[...]
        top = jnp.dot(xv, wa_ref[...], preferred_element_type=F32)
        bot = jnp.dot(xv, wb_ref[...], preferred_element_type=F32)
        pe_top = jnp.dot(pe_ref[...], wa_ref[...], preferred_element_type=F32)
        pe_bot = jnp.dot(pe_ref[...], wb_ref[...], preferred_element_type=F32)
        pre = top + pltpu.roll(bot, nc - 1, 0) + pe_top[0:1] + pe_bot[1:2]
        hid = (pre * _sigmoid(pre)).astype(BF16)
        if transposed:
            r = _nt_dot(w2_ref[...], hid)
            ones_rows = jnp.where(lax.broadcasted_iota(jnp.int32, (V_ROWS - HEAD_DIM, cc), 0) == 0, 1.0, 0.0)
            for c in range(nc // cc):
                o_ref[c] = jnp.concatenate([r[:, c * cc:(c + 1) * cc], ones_rows], axis=0).astype(BF16)
        else:
            o_ref[...] = jnp.dot(hid, w2_ref[...], preferred_element_type=F32).astype(BF16)

    if transposed:
        out_spec = pl.BlockSpec((None, nc // cc, V_ROWS, cc), lambda g: (g, 0, 0, 0))
        out_shape = jax.ShapeDtypeStruct((N_GROUPS, nc // cc, V_ROWS, cc), BF16)
    else:
        out_spec = pl.BlockSpec((None, nc, HEAD_DIM), lambda g: (g, 0, 0))
        out_shape = jax.ShapeDtypeStruct((N_GROUPS, nc, HEAD_DIM), BF16)
    return pl.pallas_call(
        body,
        grid=(N_GROUPS,),
        in_specs=[pl.BlockSpec((nc, kk), lambda g: (0, 0)),
                  pl.BlockSpec((None, kk, CMP_HIDDEN), lambda g: (g, 0, 0)),
                  pl.BlockSpec((None, kk, CMP_HIDDEN), lambda g: (g, 0, 0)),
                  pl.BlockSpec((8, kk), lambda g: (0, 0)),
                  pl.BlockSpec(w2.shape, lambda g: (0, 0))],
        out_specs=out_spec,
        out_shape=out_shape,
        compiler_params=_cparams("parallel"),
        name="compress_v" if transposed else "compress_k",
    )(xc, wa, wb, pe8, w2)


def _nsa_feature_tables(nc):
    def table(nrows, pos_step, coarse, onehots):
        n = np.arange(nrows, dtype=np.int32)
        pos = n * pos_step
        hi = (pos // coarse) * coarse
        lo = pos - hi
        cols = [hi, hi, hi, lo, lo, lo] + [np.zeros_like(n)] * (FEAT_ONEHOT - 6)
        if onehots:
            cols += [(n // SEL_BLOCK == b).astype(np.int32) for b in range(SLC_CHUNK // SEL_BLOCK)]
        cols += [np.zeros_like(n)] * (HEAD_DIM - len(cols))
        return jnp.asarray(np.stack(cols, axis=1).astype(np.float32)).astype(BF16)

    slc = table(SLC_CHUNK, 1, 16, True)
    win = table(WINDOW + Q_TILE, 1, 16, False)
    cmp_ = table(min(CMP_CHUNK, nc), CMP_STRIDE, 256, False)
    return slc, win, cmp_


def _nsa_call(qt, proj, kc, vct, vst, vwt, gates, slopes, qfeat, *, t, ks_col, kw_col):
    nq = t // Q_TILE
    n_sel = t // SEL_BLOCK
    nc = t // CMP_STRIDE
    cc = min(CMP_CHUNK, nc)
    top_n = min(SEL_TOP_N, n_sel)
    hg = HEADS_PER_GROUP
    wide = hg * Q_TILE
    sel_per_chunk = SLC_CHUNK // SEL_BLOCK
    win_span = WINDOW + Q_TILE
    kdim = 2 * HEAD_DIM
    tab_slc, tab_win, tab_cmp = _nsa_feature_tables(nc)

    def body(q_ref, kc_ref, vct_ref, ks_ref, vst_ref, kw_ref, vwt_ref, g_ref, sl_ref, qf_ref,
             ts_ref, tw_ref, tc_ref, o_ref,
             qt_ref, kss_ref, ksw_ref, ksc_ref, m_ref, acc_ref, p_ref, mw_ref, accw_ref, pw_ref,
             out_ref, s_ref, ps_ref, score_ref, sel_ref, flag_ref):
        g = pl.program_id(0)
        i = pl.program_id(1)
        t0 = i * Q_TILE
        tq = t0 + lax.broadcasted_iota(jnp.int32, (1, Q_TILE), 1)

        for h in range(hg):
            qt_ref[0:HEAD_DIM, h * Q_TILE:(h + 1) * Q_TILE] = q_ref[h * HEAD_DIM:(h + 1) * HEAD_DIM, :]
        qt_ref[HEAD_DIM:HEAD_DIM + FEAT_ONEHOT, :] = qf_ref[...]
        qt_ref[HEAD_DIM + FEAT_ONEHOT:kdim, :] = jnp.zeros((HEAD_DIM - FEAT_ONEHOT, wide), BF16)
        kss_ref[:, HEAD_DIM:kdim] = ts_ref[...]
        ksw_ref[:, HEAD_DIM:kdim] = tw_ref[...]
        ksc_ref[:, HEAD_DIM:kdim] = tc_ref[...]

        state_main = (m_ref, acc_ref, p_ref)
        state_win = (mw_ref, accw_ref, pw_ref)

        def reset(state):
            state[0][...] = jnp.full((1, wide), NEG_INF, F32)
            state[1][...] = jnp.zeros((V_ROWS, wide), F32)

        def chunk_offset(ref_pos):
            return sl_ref[...] * (ref_pos - t0).astype(F32)

        def row_blocks(nrows):
            return [(r, min(ROW_BLOCK, nrows - r)) for r in range(0, nrows, ROW_BLOCK)]

        def scores(kst_ref, nrows):
            return [jnp.dot(kst_ref[r:r + n, :], qt_ref[...], preferred_element_type=F32)
                    for r, n in row_blocks(nrows)]

        def softmax_pv(state, sts, vt_c, off, nrows, mask=None):
            m_ref, acc_ref, p_ref = state
            for (r, n), st in zip(row_blocks(nrows), sts):
                alphas = []
                keep = None if mask is None else mask(r, n)
                for h in range(hg):
                    sl = slice(h * Q_TILE, (h + 1) * Q_TILE)
                    s = st[:, sl]
                    if keep is not None:
                        s = jnp.where(keep, s, NEG_INF)
                    o = off[:, sl]
                    m_old = m_ref[:, sl]
                    m_new = jnp.maximum(m_old, jnp.max(s, axis=0, keepdims=True) + o)
                    alphas.append(jnp.exp2(m_old - m_new))
                    m_ref[:, sl] = m_new
                    p_ref[r:r + n, sl] = jnp.exp2(s - (m_new - o)).astype(BF16)
                pv = jnp.dot(vt_c[:, r:r + n], p_ref[r:r + n, :], preferred_element_type=F32)
                acc_ref[...] = acc_ref[...] * jnp.concatenate(alphas, axis=1) + pv

        def attend(state, kst_ref, vt_c, off, nrows, mask=None):
            softmax_pv(state, scores(kst_ref, nrows), vt_c, off, nrows, mask)

        def inv_norm(state, sl):
            valid = state[0][:, sl] > 0.5 * NEG_INF
            denom = state[1][HEAD_DIM:HEAD_DIM + 1, sl]
            return jnp.where(valid, 1.0 / jnp.maximum(denom, 1e-30), 0.0)

        def finish(state, branch, first):
            for h in range(hg):
                sl = slice(h * Q_TILE, (h + 1) * Q_TILE)
                gate = g_ref[pl.ds(branch * N_HEADS + g * hg + h, 1), :]
                contrib = state[1][0:HEAD_DIM, sl] * (inv_norm(state, sl) * gate)
                if first:
                    out_ref[:, sl] = contrib
                else:
                    out_ref[:, sl] += contrib

        reset(state_main)
        ps_ref[...] = jnp.zeros(ps_ref.shape, F32)
        n_cmp_keys = (t0 + Q_TILE - CMP_BLOCK) // CMP_STRIDE + 1
        n_cc = (n_cmp_keys + cc - 1) // cc

        def cmp_scores(c, carry):
            r0 = pl.multiple_of(c * cc, cc)
            n = r0 + lax.broadcasted_iota(jnp.int32, (cc, Q_TILE), 0)
            mask = (n * CMP_STRIDE + (CMP_BLOCK - 1)) <= tq
            off = chunk_offset(r0 * CMP_STRIDE + (CMP_BLOCK - 1))
            ksc_ref[:, 0:HEAD_DIM] = kc_ref[pl.ds(r0, cc), :]
            for hp in range(hg // 2):
                pair = slice(hp * 2 * Q_TILE, (hp + 1) * 2 * Q_TILE)
                st = jnp.dot(ksc_ref[...], qt_ref[:, pair], preferred_element_type=F32)
                for hh in range(2):
                    sl = slice((2 * hp + hh) * Q_TILE, (2 * hp + hh + 1) * Q_TILE)
                    s = jnp.where(mask, st[:, hh * Q_TILE:(hh + 1) * Q_TILE] + off[:, sl], NEG_INF)
                    s_ref[pl.ds(r0, cc), sl] = s
                    m_ref[:, sl] = jnp.maximum(m_ref[:, sl], jnp.max(s, axis=0, keepdims=True))
            return carry

        lax.fori_loop(0, n_cc, cmp_scores, 0)

        def cmp_probs(c, carry):
            r0 = pl.multiple_of(c * cc, cc)
            for hp in range(hg // 2):
                pair = slice(hp * 2 * Q_TILE, (hp + 1) * 2 * Q_TILE)
                for hh in range(2):
                    sl = slice((2 * hp + hh) * Q_TILE, (2 * hp + hh + 1) * Q_TILE)
                    p = jnp.exp2(s_ref[pl.ds(r0, cc), sl] - m_ref[:, sl])
                    s_ref[pl.ds(r0, cc), sl] = p
                    p_ref[0:cc, sl] = p.astype(BF16)
                acc_ref[:, pair] += jnp.dot(vct_ref[c], p_ref[0:cc, pair], preferred_element_type=F32)
            return carry

        lax.fori_loop(0, n_cc, cmp_probs, 0)
        finish(state_main, 0, True)

        def ps_body(c, carry):
            r0 = pl.multiple_of(c * cc, cc)
            tot = jnp.zeros((cc, Q_TILE), F32)
            for h in range(hg):
                sl = slice(h * Q_TILE, (h + 1) * Q_TILE)
                tot = tot + s_ref[pl.ds(r0, cc), sl] * inv_norm(state_main, sl)
            ps_ref[pl.ds(8 + r0, cc), :] = tot
            return carry

        lax.fori_loop(0, n_cc, ps_body, 0)

        ratio = SEL_BLOCK // CMP_STRIDE
        imp = ps_ref[pl.ds(7, n_sel, stride=ratio), :]
        for kk in range(ratio):
            imp = imp + ps_ref[pl.ds(8 + kk, n_sel, stride=ratio), :]
        jrow = lax.broadcasted_iota(jnp.int32, (n_sel, Q_TILE), 0)
        cur = tq // SEL_BLOCK
        forced = jnp.where(jrow == 0, 1.0, jnp.where(jrow == cur, 1.0, jnp.where(jrow == cur - 1, 1.0, 0.0)))
        score_ref[...] = jnp.where(forced > 0.5, -2.0, jnp.where(jrow <= cur, imp, -1.0))
        sel_ref[...] = forced
        jrow_f = jrow.astype(F32)

        def run_topk(nrows):
            jr = jrow_f[0:nrows, :]

            def topk_body(r, carry):
                sc = score_ref[0:nrows, :]
                mx = jnp.max(sc, axis=0, keepdims=True)
                first = jnp.min(jnp.where(sc == mx, jr, float(n_sel)), axis=0, keepdims=True)
                pick = jr == first
                sel_ref[0:nrows, :] = jnp.where(pick, 1.0, sel_ref[0:nrows, :])
                score_ref[0:nrows, :] = jnp.where(pick, -2.0, sc)
                return carry

            lax.fori_loop(0, top_n - N_FORCED, topk_body, 0)

        half_rows = n_sel // 2
        tile_in_first_half = (t0 + Q_TILE - 1) // SEL_BLOCK < half_rows

        @pl.when(tile_in_first_half)
        def _():
            run_topk(half_rows)

        @pl.when(jnp.logical_not(tile_in_first_half))
        def _():
            run_topk(n_sel)
        for c in range(t // SLC_CHUNK):
            blocks = sel_ref[c * sel_per_chunk:(c + 1) * sel_per_chunk, :]
            flag_ref[c] = (jnp.max(blocks) > 0.5).astype(jnp.int32)

        reset(state_main)
        last_sc = (t0 + Q_TILE - 1) // SLC_CHUNK

        def slc_stage(c):
            r0 = pl.multiple_of(c * SLC_CHUNK, SLC_CHUNK)
            kss_ref[:, 0:HEAD_DIM] = ks_ref[pl.ds(r0, SLC_CHUNK), :]
            sel8 = sel_ref[pl.ds(pl.multiple_of(c * sel_per_chunk, sel_per_chunk), sel_per_chunk), :]
            bias = (sel8 - 1.0) * MASK_BIG
            rows = jnp.concatenate([jnp.concatenate([bias] * hg, axis=1),
                                    jnp.zeros((FEAT_ONEHOT - sel_per_chunk, wide), F32)], axis=0)
            qt_ref[HEAD_DIM + FEAT_ONEHOT:HEAD_DIM + 2 * FEAT_ONEHOT, :] = rows.astype(BF16)
            return r0

        def slc_body(c, carry):
            @pl.when(flag_ref[c] > 0)
            def _():
                r0 = slc_stage(c)
                attend(state_main, kss_ref, vst_ref[c], chunk_offset(r0), SLC_CHUNK)

            return carry

        lax.fori_loop(0, last_sc, slc_body, 0)

        reset(state_win)
        r0s = slc_stage(last_sc)
        n_wc = win_span // WIN_CHUNK
        c0 = jnp.maximum(i * (Q_TILE // WIN_CHUNK) - WINDOW // WIN_CHUNK, 0)
        r0w = pl.multiple_of(c0 * WIN_CHUNK, WIN_CHUNK)
        ksw_ref[:, 0:HEAD_DIM] = kw_ref[pl.ds(r0w, win_span), :]
        sts_slc = scores(kss_ref, SLC_CHUNK)
        sts_win = scores(ksw_ref, win_span)

        def masks(r, n):
            return (r0s + r + lax.broadcasted_iota(jnp.int32, (n, Q_TILE), 0)) <= tq

        softmax_pv(state_main, sts_slc, vst_ref[last_sc], chunk_offset(r0s), SLC_CHUNK, masks)
        finish(state_main, 1, False)

        def maskw(r, n):
            dpos = (r0w + r + lax.broadcasted_iota(jnp.int32, (n, Q_TILE), 0)) - tq
            return jnp.where(dpos <= 0, dpos, -2 * WINDOW) > -WINDOW

        vtw = jnp.concatenate([vwt_ref[c0 + kk] for kk in range(n_wc)], axis=1)
        softmax_pv(state_win, sts_win, vtw, chunk_offset(r0w), win_span, maskw)
        finish(state_win, 2, False)

        for h in range(hg):
            sl = slice(h * Q_TILE, (h + 1) * Q_TILE)
            o_ref[:, h * HEAD_DIM:(h + 1) * HEAD_DIM] = out_ref[:, sl].T.astype(o_ref.dtype)

    one = pl.Buffered(1)
    in_specs = [
        pl.BlockSpec((hg * HEAD_DIM, Q_TILE), lambda g, i: (g, i)),
        pl.BlockSpec((None, nc, HEAD_DIM), lambda g, i: (g, 0, 0)),
        pl.BlockSpec((None, nc // cc, V_ROWS, cc), lambda g, i: (g, 0, 0, 0)),
        pl.BlockSpec((t, HEAD_DIM), lambda g, i: (0, ks_col + g), pipeline_mode=one),
        pl.BlockSpec((t // SLC_CHUNK, V_ROWS, SLC_CHUNK), lambda g, i: (0, g, 0), pipeline_mode=one),
        pl.BlockSpec((t, HEAD_DIM), lambda g, i: (0, kw_col + g), pipeline_mode=one),
        pl.BlockSpec((t // WIN_CHUNK, V_ROWS, WIN_CHUNK), lambda g, i: (0, g, 0), pipeline_mode=one),
        pl.BlockSpec((gates.shape[0], Q_TILE), lambda g, i: (0, i)),
        pl.BlockSpec((None, 1, wide), lambda g, i: (g, 0, 0)),
        pl.BlockSpec((None, FEAT_ONEHOT, wide), lambda g, i: (g, 0, 0)),
        pl.BlockSpec(tab_slc.shape, lambda g, i: (0, 0)),
        pl.BlockSpec(tab_win.shape, lambda g, i: (0, 0)),
        pl.BlockSpec(tab_cmp.shape, lambda g, i: (0, 0)),
    ]
    scratch = [
        pltpu.VMEM((kdim, wide), BF16),
        pltpu.VMEM((SLC_CHUNK, kdim), BF16),
        pltpu.VMEM((win_span, kdim), BF16),
        pltpu.VMEM((cc, kdim), BF16),
        pltpu.VMEM((1, wide), F32),
        pltpu.VMEM((V_ROWS, wide), F32),
        pltpu.VMEM((SLC_CHUNK, wide), BF16),
        pltpu.VMEM((1, wide), F32),
        pltpu.VMEM((V_ROWS, wide), F32),
        pltpu.VMEM((win_span, wide), BF16),
        pltpu.VMEM((HEAD_DIM, wide), F32),
        pltpu.VMEM((nc, wide), F32),
        pltpu.VMEM((8 + nc, Q_TILE), F32),
        pltpu.VMEM((n_sel, Q_TILE), F32),
        pltpu.VMEM((n_sel, Q_TILE), F32),
        pltpu.SMEM((t // SLC_CHUNK,), jnp.int32),
    ]
    return pl.pallas_call(
        body,
        grid=(N_GROUPS, nq),
        in_specs=in_specs,
        out_specs=pl.BlockSpec((Q_TILE, hg * HEAD_DIM), lambda g, i: (i, g)),
        out_shape=jax.ShapeDtypeStruct((t, N_HEADS * HEAD_DIM), BF16),
        scratch_shapes=scratch,
        compiler_params=_cparams("parallel", "parallel"),
        name="nsa",
    )(qt, kc, vct, proj, vst, proj, vwt, gates, slopes, qfeat, tab_slc, tab_win, tab_cmp)


def _glu_proj_call(h, w_glu):
    t, d = h.shape
    ch = w_glu.shape[1] // 2
    tm = min(1024, t)
    tn = 512
    nj = ch // tn

    def body(h_ref, wa_ref, wg_ref, o_ref):
        hv = h_ref[...]
        a = jnp.dot(hv, wa_ref[...], preferred_element_type=F32)
        g = jnp.dot(hv, wg_ref[...], preferred_element_type=F32)
        o_ref[...] = (a * _sigmoid(g)).astype(o_ref.dtype)

    return pl.pallas_call(
        body,
        grid=(t // tm, nj),
        in_specs=[pl.BlockSpec((tm, d), lambda i, j: (i, 0)),
                  pl.BlockSpec((d, tn), lambda i, j: (0, j)),
                  pl.BlockSpec((d, tn), lambda i, j: (0, nj + j))],
        out_specs=pl.BlockSpec((tm, tn), lambda i, j: (i, j)),
        out_shape=jax.ShapeDtypeStruct((t, ch), BF16),
        compiler_params=_cparams("parallel", "parallel"),
        name="proj_glu",
    )(h, w_glu, w_glu)


def _conv_front_call(u, w_dw, b_dw, ln_g, ln_b):
    t = u.shape[0]
    ch = w_dw.shape[1]
    tm = min(256, t)
    halo = 32
    rb = 32
    strip = 512
    w_rep = jnp.repeat(w_dw, F32_SUBLANES, axis=0)

    def body(a_ref, ah_ref, w_ref, b_ref, lg_ref, lb_ref, o_ref, u_ref, sb_ref, y_ref):
        i = pl.program_id(0)
        u_ref[halo:halo + tm, :] = a_ref[...].astype(F32)
        u_ref[0:halo, :] = jnp.where(i > 0, ah_ref[...].astype(F32), 0.0)
        for cs in range(ch // strip):
            cols = slice(cs * strip, (cs + 1) * strip)
            ue = u_ref[:, cols]
            sb_ref[0] = ue
            for part in range(1, F32_SUBLANES):
                sb_ref[part] = pltpu.roll(ue, part, 0)

            def rows(r, carry, cols=cols):
                r0 = pl.multiple_of(r * rb, rb)
                acc = jnp.zeros((rb // F32_SUBLANES, F32_SUBLANES, strip), F32)
                for s in range(CONV_WIDTH):
                    whole, part = divmod(s, F32_SUBLANES)
                    k = CONV_WIDTH - 1 - s
                    tap = sb_ref[part, pl.ds(halo + r0 - F32_SUBLANES * whole, rb), :]
                    wk = w_ref[k * F32_SUBLANES:(k + 1) * F32_SUBLANES, cols]
                    acc = acc + tap.reshape(rb // F32_SUBLANES, F32_SUBLANES, strip) * wk
                y_ref[pl.ds(r0, rb), cols] = acc.reshape(rb, strip) + b_ref[:, cols]
                return carry

            lax.fori_loop(0, tm // rb, rows, 0)
        y = y_ref[...]
        mu = jnp.mean(y, axis=-1, keepdims=True)
        yc = y - mu
        var = jnp.mean(yc * yc, axis=-1, keepdims=True)
        z = yc * lax.rsqrt(var + EPS) * lg_ref[...] + lb_ref[...]
        o_ref[...] = (z * _sigmoid(z)).astype(o_ref.dtype)

    per = tm // halo
    return pl.pallas_call(
        body,
        grid=(t // tm,),
        in_specs=[pl.BlockSpec((tm, ch), lambda i: (i, 0)),
                  pl.BlockSpec((halo, ch), lambda i: (jnp.maximum(i * per - 1, 0), 0)),
                  pl.BlockSpec((CONV_WIDTH * F32_SUBLANES, ch), lambda i: (0, 0)),
                  pl.BlockSpec((1, ch), lambda i: (0, 0)),
                  pl.BlockSpec((1, ch), lambda i: (0, 0)),
                  pl.BlockSpec((1, ch), lambda i: (0, 0))],
        out_specs=pl.BlockSpec((tm, ch), lambda i: (i, 0)),
        out_shape=jax.ShapeDtypeStruct((t, ch), BF16),
        scratch_shapes=[pltpu.VMEM((halo + tm, ch), F32),
                        pltpu.VMEM((F32_SUBLANES, halo + tm, strip), F32),
                        pltpu.VMEM((tm, ch), F32)],
        compiler_params=_cparams("parallel"),
        name="conv_front",
    )(u, u, w_rep, b_dw.reshape(1, ch), ln_g.reshape(1, ch), ln_b.reshape(1, ch))


def _ffn_up_call(h2, w_up, w_dw, b_dw):
    t, d = h2.shape
    n2 = w_up.shape[1]
    f = n2 // 2
    tm = min(1024, t)
    rsub = 256
    tn = 512
    nj = f // tn
    sub = F32_SUBLANES
    w_pad = jnp.concatenate([w_dw, jnp.zeros((sub - FFN_CONV_WIDTH, n2), F32)], axis=0)
    b2 = b_dw.reshape(1, n2)

    def body(h_ref, wa_ref, wg_ref, da_ref, dg_ref, ba_ref, bg_ref, o_ref, ca_ref, cg_ref):
        i = pl.program_id(0)
        j = pl.program_id(1)

        @pl.when(i == 0)
        def _():
            ca_ref[j] = jnp.zeros((sub, tn), F32)
            cg_ref[j] = jnp.zeros((sub, tn), F32)

        wa = wa_ref[...]
        wg = wg_ref[...]

        def conv(up, prev, d_ref, b_ref):
            ext = jnp.concatenate([prev, up], axis=0)
            acc = b_ref[...] + up * d_ref[FFN_CONV_WIDTH - 1:FFN_CONV_WIDTH, :]
            for s in range(1, FFN_CONV_WIDTH):
                k = FFN_CONV_WIDTH - 1 - s
                acc = acc + pltpu.roll(ext, s, 0)[sub:, :] * d_ref[k:k + 1, :]
            return acc

        def project(rs):
            hv = h_ref[rs * rsub:(rs + 1) * rsub, :]
            return (jnp.dot(hv, wa, preferred_element_type=F32),
                    jnp.dot(hv, wg, preferred_element_type=F32))

        prev_a = ca_ref[j]
        prev_g = cg_ref[j]
        n_rs = tm // rsub
        ups = project(0)
        for rs in range(n_rs):
            rows = slice(rs * rsub, (rs + 1) * rsub)
            up_a, up_g = ups
            if rs + 1 < n_rs:
                ups = project(rs + 1)
            ca = conv(up_a, prev_a, da_ref, ba_ref)
            cg = conv(up_g, prev_g, dg_ref, bg_ref)
            o_ref[rows, :] = (cg * _sigmoid(cg) * ca).astype(o_ref.dtype)
            prev_a = up_a[rsub - sub:, :]
            prev_g = up_g[rsub - sub:, :]
        ca_ref[j] = prev_a
        cg_ref[j] = prev_g

    return pl.pallas_call(
        body,
        grid=(t // tm, nj),
        in_specs=[pl.BlockSpec((tm, d), lambda i, j: (i, 0)),
                  pl.BlockSpec((d, tn), lambda i, j: (0, j)),
                  pl.BlockSpec((d, tn), lambda i, j: (0, nj + j)),
                  pl.BlockSpec((sub, tn), lambda i, j: (0, j)),
                  pl.BlockSpec((sub, tn), lambda i, j: (0, nj + j)),
                  pl.BlockSpec((1, tn), lambda i, j: (0, j)),
                  pl.BlockSpec((1, tn), lambda i, j: (0, nj + j))],
        out_specs=pl.BlockSpec((tm, tn), lambda i, j: (i, j)),
        out_shape=jax.ShapeDtypeStruct((t, f), BF16),
        scratch_shapes=[pltpu.VMEM((nj, sub, tn), F32), pltpu.VMEM((nj, sub, tn), F32)],
        compiler_params=_cparams("arbitrary", "arbitrary"),
        name="ffn_up",
    )(h2, w_up, w_up, w_pad, w_pad, b2, b2)


def _layer(x, c_row, p):
    t, d = x.shape
    ada = _ada_call(c_row, p["w_ada"], p["b_ada"])

    w_in = p["w_in"]
    qw = N_HEADS * HEAD_DIM
    kvw = N_GROUPS * HEAD_DIM
    o = [0, qw]
    for _ in range(6):
        o.append(o[-1] + kvw)
    o.append(o[-1] + 3 * N_HEADS)
    o.append(o[-1] + 2 * d)
    o.append(o[-1] + 2 * d)
    w_q, w_kc, w_vc, w_ks, w_vs, w_kw, w_vw, w_gn, w_glu, w_mg = [w_in[:, o[k]:o[k + 1]] for k in range(10)]
    w_rm = jnp.concatenate([w_kc, w_vc, w_ks, w_kw], axis=1).astype(BF16)
    w_glu = w_glu.astype(BF16)
    w_mg = w_mg.astype(BF16)
    w_qt = w_q.T.astype(BF16)
    w_vt = jnp.concatenate([w_vs, w_vw], axis=1).T.astype(BF16)
    n_gate_rows = 64
    w_gt = jnp.concatenate([w_gn.T, jnp.zeros((n_gate_rows - 3 * N_HEADS, d), F32)], axis=0).astype(BF16)

    h1 = _norm_call(x, p["norm1_g"], ada, shift_idx=0, scale_idx=1)
    proj = _matmul_call(h1, w_rm, tm=1024, tn=4 * kvw, out_dtype=BF16, name="proj_rm")
    u = _glu_proj_call(h1, w_glu)
    gm = _matmul_call(h1, w_mg, tm=1024, tn=1024, out_dtype=BF16, name="proj_mg")
    qt, vst, vwt, gates = _nt_proj_call(h1, w_qt, w_vt, w_gt, HEAD_DIM ** -0.5 * LOG2E)

    nc = t // CMP_STRIDE
    cc = min(CMP_CHUNK, nc)
    half = CMP_BLOCK // CMP_STRIDE
    assert half == 2

    def expand(w1):
        w1r = w1.reshape(half, CMP_STRIDE, HEAD_DIM, CMP_HIDDEN)
        outs = []
        for hf in range(half):
            per_g = []
            for g in range(N_GROUPS):
                z = jnp.zeros((CMP_STRIDE, N_GROUPS, HEAD_DIM, CMP_HIDDEN), F32).at[:, g].set(w1r[hf])
                per_g.append(z.reshape(CMP_STRIDE * kvw, CMP_HIDDEN))
            outs.append(jnp.stack(per_g).astype(BF16))
        return outs

    pe = p["cmp_pe"].reshape(half, CMP_STRIDE, 1, HEAD_DIM)
    pe_rows = jnp.broadcast_to(pe, (half, CMP_STRIDE, N_GROUPS, HEAD_DIM)).reshape(half, CMP_STRIDE * kvw)
    pe8 = jnp.concatenate([pe_rows, jnp.zeros((8 - half, CMP_STRIDE * kvw), F32)], axis=0).astype(BF16)
    kc_raw = proj[:, 0:kvw].reshape(nc, CMP_STRIDE * kvw)
    vc_raw = proj[:, kvw:2 * kvw].reshape(nc, CMP_STRIDE * kvw)
    wka, wkb = expand(p["w_kc1"])
    wva, wvb = expand(p["w_vc1"])
    kc = _compress_call(kc_raw, wka, wkb, pe8, p["w_kc2"].astype(BF16), transposed=False, cc=cc)
    vct = _compress_call(vc_raw, wva, wvb, pe8, p["w_vc2"].T.astype(BF16), transposed=True, cc=cc)

    head_ids = np.arange(1, N_HEADS + 1, dtype=np.float64)
    slopes_np = (np.exp2(-8.0 * head_ids / N_HEADS) * LOG2E).astype(np.float32)
    slopes_np = np.broadcast_to(slopes_np.reshape(N_GROUPS, 1, HEADS_PER_GROUP, 1),
                                (N_GROUPS, 1, HEADS_PER_GROUP, Q_TILE)).reshape(N_GROUPS, 1, HEADS_PER_GROUP * Q_TILE)
    s1 = slopes_np.astype(BF16)
    s2 = (slopes_np - s1.astype(np.float32)).astype(BF16)
    s3 = (slopes_np - s1.astype(np.float32) - s2.astype(np.float32)).astype(BF16)
    zero_rows = np.zeros((N_GROUPS, FEAT_ONEHOT - 6, slopes_np.shape[-1]), dtype=s1.dtype)
    qfeat = jnp.asarray(np.concatenate([s1, s2, s3, s1, s2, s3, zero_rows], axis=1))
    slopes = jnp.asarray(slopes_np)
    o_nsa = _nsa_call(qt, proj, kc, vct, vst, vwt, gates, slopes, qfeat, t=t,
                      ks_col=(2 * kvw) // HEAD_DIM, kw_col=(3 * kvw) // HEAD_DIM)

    u2 = _conv_front_call(u, p["conv_dw_w"], p["conv_dw_b"], p["conv_ln_g"], p["conv_ln_b"])

    tn = 1024
    nb = d // tn
    ya = _matmul_call(o_nsa, p["w_o_nsa"].astype(BF16), tm=512, tn=tn, out_dtype=F32, name="o_proj",
                      epilogue=lambda acc, ga_ref: acc * _sigmoid(ga_ref[...].astype(F32)),
                      extras=(gm,), extra_specs=(pl.BlockSpec((512, tn), lambda i, j: (i, j)),))
    merged = _matmul_call(
        u2, p["conv_pw_w"].astype(BF16), tm=512, tn=tn, out_dtype=BF16, name="pw_merge",
        epilogue=lambda acc, b_ref, gb_ref, ya_ref: (acc + b_ref[...]) * _sigmoid(gb_ref[...].astype(F32)) + ya_ref[...],
        extras=(p["conv_pw_b"].reshape(1, d), gm, ya),
        extra_specs=(pl.BlockSpec((1, tn), lambda i, j: (0, j)),
                     pl.BlockSpec((512, tn), lambda i, j: (i, nb + j)),
                     pl.BlockSpec((512, tn), lambda i, j: (i, j))))
    x1, h2 = _out_norm_call(merged, p["w_out"].astype(BF16), x, ada, p["norm2_g"],
                            gate_idx=2, shift_idx=3, scale_idx=4)
    act = _ffn_up_call(h2, p["ffn_w_up"].astype(BF16), p["ffn_dw_w"], p["ffn_dw_b"])
    x2 = _matmul_call(
        act, p["ffn_w_down"].astype(BF16), tm=512, tn=512, out_dtype=F32, name="ffn_down",
        epilogue=lambda acc, x_ref, g_ref: x_ref[...] + g_ref[0:1, :] * acc,
        extras=(x1, ada),
        extra_specs=(pl.BlockSpec((512, 512), lambda i, j: (i, j)),
                     pl.BlockSpec((8, 512), lambda i, j: (0, 5 * (d // 512) + j))))
    return x2


def kernel(x, c, w_ada, b_ada, norm1_g, w_in, cmp_pe, w_kc1, w_kc2, w_vc1, w_vc2, w_o_nsa, conv_dw_w, conv_dw_b, conv_ln_g, conv_ln_b, conv_pw_w, conv_pw_b, w_out, norm2_g, ffn_w_up, ffn_dw_w, ffn_dw_b, ffn_w_down, final_g):
    b, t, d = x.shape
    stacked = dict(w_ada=w_ada, b_ada=b_ada, norm1_g=norm1_g, w_in=w_in, cmp_pe=cmp_pe, w_kc1=w_kc1,
                   w_kc2=w_kc2, w_vc1=w_vc1, w_vc2=w_vc2, w_o_nsa=w_o_nsa, conv_dw_w=conv_dw_w,
                   conv_dw_b=conv_dw_b, conv_ln_g=conv_ln_g, conv_ln_b=conv_ln_b, conv_pw_w=conv_pw_w,
                   conv_pw_b=conv_pw_b, w_out=w_out, norm2_g=norm2_g, ffn_w_up=ffn_w_up,
                   ffn_dw_w=ffn_dw_w, ffn_dw_b=ffn_dw_b, ffn_w_down=ffn_w_down)
    depth = w_ada.shape[0]
    xs = x.reshape(b * t, d)
    outs = []
    for bi in range(b):
        xb = xs if b == 1 else xs[bi * t:(bi + 1) * t]
        for layer in range(depth):
            xb = _layer(xb, c[bi:bi + 1], {k: v[layer] for k, v in stacked.items()})
        outs.append(_norm_call(xb, final_g, out_dtype=x.dtype))
    out = outs[0] if b == 1 else jnp.concatenate(outs, axis=0)
    return out.reshape(b, t, d)
```

```python
import functools

import jax
import jax.numpy as jnp
import numpy as np
from jax import lax
from jax.experimental import pallas as pl
from jax.experimental.pallas import tpu as pltpu

F32 = jnp.float32
BF16 = jnp.bfloat16

N_HEADS = 16
N_GROUPS = 2
HEADS_PER_GROUP = N_HEADS // N_GROUPS
HEAD_DIM = 128
CMP_BLOCK = 32
CMP_STRIDE = 16
CMP_HIDDEN = 2 * HEAD_DIM
SEL_BLOCK = 64
SEL_TOP_N = 16
WINDOW = 512
FORCE_BONUS = 1e6
CONV_WIDTH = 31
FFN_CONV_WIDTH = 3
EPS = 1e-6
NEG_INF = -1e30
LOG2E = 1.4426950408889634

V7X_VMEM_BYTES = 64 * 1024 * 1024
VMEM_LIMIT_BYTES = V7X_VMEM_BYTES - 8 * 1024 * 1024
LANES = 128
F32_SUBLANES = 8

Q_TILE = 128
SLC_CHUNK = 512
WIN_CHUNK = 128
CMP_CHUNK = 256
FEAT_ONEHOT = 16
MASK_BIG = 2.0 ** 20
N_FORCED = 3
ROW_BLOCK = 256
V_ROWS = HEAD_DIM + 16


def _cparams(*sem):
    return pltpu.CompilerParams(dimension_semantics=sem, vmem_limit_bytes=VMEM_LIMIT_BYTES)


def _sigmoid(v):
    return 1.0 / (1.0 + jnp.exp(-v))


def _nt_dot(a, b):
    return lax.dot_general(a, b, (((1,), (1,)), ((), ())), preferred_element_type=F32)


def _ada_call(c_row, w, b):
    d, n = w.shape
    tn = 1024
    cb = jnp.broadcast_to(c_row.reshape(d, 1), (d, LANES))

    def body(cb_ref, w_ref, b_ref, o_ref):
        cv = cb_ref[...]
        act = cv * _sigmoid(cv)
        for j in range(tn // LANES):
            sl = slice(j * LANES, (j + 1) * LANES)
            row = jnp.sum(w_ref[:, sl] * act, axis=0, keepdims=True) + b_ref[:, sl]
            o_ref[:, sl] = jnp.broadcast_to(row, (8, LANES))

    return pl.pallas_call(
        body,
        grid=(n // tn,),
        in_specs=[pl.BlockSpec((d, LANES), lambda j: (0, 0)),
                  pl.BlockSpec((d, tn), lambda j: (0, j)),
                  pl.BlockSpec((1, tn), lambda j: (0, j))],
        out_specs=pl.BlockSpec((8, tn), lambda j: (0, j)),
        out_shape=jax.ShapeDtypeStruct((8, n), F32),
        compiler_params=_cparams("parallel"),
        name="ada",
    )(cb, w, b.reshape(1, n))


def _norm_call(x, gain, ada=None, shift_idx=0, scale_idx=0, out_dtype=BF16):
    t, d = x.shape
    tm = 512

    def body(*refs):
        if ada is None:
            x_ref, g_ref, o_ref = refs
        else:
            x_ref, g_ref, sh_ref, sc_ref, o_ref = refs
        xv = x_ref[...]
        y = xv * lax.rsqrt(jnp.mean(xv * xv, axis=-1, keepdims=True) + EPS) * g_ref[...]
        if ada is not None:
            y = y * (1.0 + sc_ref[0:1, :]) + sh_ref[0:1, :]
        o_ref[...] = y.astype(o_ref.dtype)

    in_specs = [pl.BlockSpec((tm, d), lambda i: (i, 0)), pl.BlockSpec((1, d), lambda i: (0, 0))]
    args = [x, gain.reshape(1, d)]
    if ada is not None:
        in_specs += [pl.BlockSpec((8, d), lambda i: (0, shift_idx)),
                     pl.BlockSpec((8, d), lambda i: (0, scale_idx))]
        args += [ada, ada]
    return pl.pallas_call(
        body,
        grid=(t // tm,),
        in_specs=in_specs,
        out_specs=pl.BlockSpec((tm, d), lambda i: (i, 0)),
        out_shape=jax.ShapeDtypeStruct((t, d), out_dtype),
        compiler_params=_cparams("parallel"),
        name="norm",
    )(*args)


def _matmul_call(a, w, *, tm, tn, out_dtype, name, epilogue=None, extras=(), extra_specs=()):
    m, k = a.shape
    n = w.shape[1]
    tm = min(tm, m)

    def body(a_ref, w_ref, *rest):
        o_ref = rest[-1]
        acc = jnp.dot(a_ref[...], w_ref[...].astype(BF16), preferred_element_type=F32)
        if epilogue is not None:
            acc = epilogue(acc, *rest[:-1])
        o_ref[...] = acc.astype(o_ref.dtype)

    return pl.pallas_call(
        body,
        grid=(m // tm, n // tn),
        in_specs=[pl.BlockSpec((tm, k), lambda i, j: (i, 0)),
                  pl.BlockSpec((k, tn), lambda i, j: (0, j))] + list(extra_specs),
        out_specs=pl.BlockSpec((tm, tn), lambda i, j: (i, j)),
        out_shape=jax.ShapeDtypeStruct((m, n), out_dtype),
        compiler_params=_cparams("parallel", "parallel"),
        name=name,
    )(a, w, *extras)


def _transpose_cast_call(w, col_ranges):
    k = w.shape[0]
    tc = 256
    blocks = []
    for start, width in col_ranges:
        assert start % tc == 0 and width % tc == 0, (start, width)
        blocks += [start // tc + b for b in range(width // tc)]
    n_blocks = len(blocks)

    def col_block(i):
        idx = jnp.int32(blocks[0])
        for pos in range(1, n_blocks):
            idx = jnp.where(i >= pos, jnp.int32(blocks[pos]), idx)
        return idx

    def body(w_ref, o_ref):
        o_ref[...] = w_ref[...].T.astype(o_ref.dtype)

    return pl.pallas_call(
        body,
        grid=(n_blocks,),
        in_specs=[pl.BlockSpec((k, tc), lambda i: (0, col_block(i)))],
        out_specs=pl.BlockSpec((tc, k), lambda i: (i, 0)),
        out_shape=jax.ShapeDtypeStruct((n_blocks * tc, k), BF16),
        compiler_params=_cparams("parallel"),
        name="transpose_cast",
    )(w)


def _merge_call(a, wa, b, wb, bias, gates, *, gate_col):
    t, ka = a.shape
    kb = b.shape[1]
    d = wa.shape[1]
    tm = min(512, t)
    tn = 1024
    nb = d // tn
    g0 = gate_col // tn

    def body(a_ref, wa_ref, b_ref, wb_ref, bias_ref, ga_ref, gb_ref, o_ref):
        ya = jnp.dot(a_ref[...], wa_ref[...], preferred_element_type=F32)
        yb = jnp.dot(b_ref[...], wb_ref[...], preferred_element_type=F32) + bias_ref[...]
        o_ref[...] = (_sigmoid(ga_ref[...].astype(F32)) * ya
                      + _sigmoid(gb_ref[...].astype(F32)) * yb).astype(o_ref.dtype)

    return pl.pallas_call(
        body,
        grid=(t // tm, nb),
        in_specs=[pl.BlockSpec((tm, ka), lambda i, j: (i, 0)),
                  pl.BlockSpec((ka, tn), lambda i, j: (0, j)),
                  pl.BlockSpec((tm, kb), lambda i, j: (i, 0)),
                  pl.BlockSpec((kb, tn), lambda i, j: (0, j)),
                  pl.BlockSpec((1, tn), lambda i, j: (0, j)),
                  pl.BlockSpec((tm, tn), lambda i, j: (i, g0 + j)),
                  pl.BlockSpec((tm, tn), lambda i, j: (i, g0 + nb + j))],
        out_specs=pl.BlockSpec((tm, tn), lambda i, j: (i, j)),
        out_shape=jax.ShapeDtypeStruct((t, d), BF16),
        compiler_params=_cparams("parallel", "parallel"),
        name="merge",
    )(a, wa, b, wb, bias.reshape(1, d), gates, gates)


def _out_norm_call(a, w, x, ada, gain, *, gate_idx, shift_idx, scale_idx):
    t, d = x.shape
    tm = min(512, t)

    def body(a_ref, w_ref, x_ref, g_ref, sh_ref, sc_ref, n_ref, x1_ref, h_ref):
        acc = jnp.dot(a_ref[...], w_ref[...], preferred_element_type=F32)
        x1 = x_ref[...] + g_ref[0:1, :] * acc
        x1_ref[...] = x1
        y = x1 * lax.rsqrt(jnp.mean(x1 * x1, axis=-1, keepdims=True) + EPS) * n_ref[...]
        h_ref[...] = (y * (1.0 + sc_ref[0:1, :]) + sh_ref[0:1, :]).astype(h_ref.dtype)

    return pl.pallas_call(
        body,
        grid=(t // tm,),
        in_specs=[pl.BlockSpec((tm, a.shape[1]), lambda i: (i, 0)),
                  pl.BlockSpec(w.shape, lambda i: (0, 0), pipeline_mode=pl.Buffered(1)),
                  pl.BlockSpec((tm, d), lambda i: (i, 0)),
                  pl.BlockSpec((8, d), lambda i: (0, gate_idx)),
                  pl.BlockSpec((8, d), lambda i: (0, shift_idx)),
                  pl.BlockSpec((8, d), lambda i: (0, scale_idx)),
                  pl.BlockSpec((1, d), lambda i: (0, 0))],
        out_specs=[pl.BlockSpec((tm, d), lambda i: (i, 0)),
                   pl.BlockSpec((tm, d), lambda i: (i, 0))],
        out_shape=[jax.ShapeDtypeStruct((t, d), F32), jax.ShapeDtypeStruct((t, d), BF16)],
        compiler_params=_cparams("parallel"),
        name="out_proj_norm",
    )(a, w, x, ada, ada, ada, gain.reshape(1, d))


def _nt_proj_call(h, w_qt, w_vt, w_gt, q_scale):
    t, d = h.shape
    tq = SLC_CHUNK
    nqr = w_qt.shape[0]
    nv = w_vt.shape[0] // 2
    ng = w_gt.shape[0]
    per = tq // WIN_CHUNK

    def body(h_ref, wq_ref, wv_ref, wg_ref, q_ref, vs_ref, vw_ref, g_ref):
        hv = h_ref[...]
        q_ref[...] = (_nt_dot(wq_ref[...], hv) * q_scale).astype(BF16)
        r = _nt_dot(wv_ref[...], hv)
        ones_rows = jnp.where(lax.broadcasted_iota(jnp.int32, (V_ROWS - HEAD_DIM, tq), 0) == 0, 1.0, 0.0)
        for g in range(N_GROUPS):
            full_s = jnp.concatenate([r[g * HEAD_DIM:(g + 1) * HEAD_DIM], ones_rows], axis=0).astype(BF16)
            vs_ref[0, g * V_ROWS:(g + 1) * V_ROWS, :] = full_s
            full_w = jnp.concatenate([r[nv + g * HEAD_DIM:nv + (g + 1) * HEAD_DIM], ones_rows], axis=0).astype(BF16)
            for kk in range(per):
                vw_ref[kk, g * V_ROWS:(g + 1) * V_ROWS, :] = full_w[:, kk * WIN_CHUNK:(kk + 1) * WIN_CHUNK]
        g_ref[...] = _sigmoid(_nt_dot(wg_ref[...], hv))

    return pl.pallas_call(
        body,
        grid=(t // tq,),
        in_specs=[pl.BlockSpec((tq, d), lambda i: (i, 0)),
                  pl.BlockSpec((nqr, d), lambda i: (0, 0)),
                  pl.BlockSpec((2 * nv, d), lambda i: (0, 0)),
                  pl.BlockSpec((ng, d), lambda i: (0, 0))],
        out_specs=[pl.BlockSpec((nqr, tq), lambda i: (0, i)),
                   pl.BlockSpec((1, N_GROUPS * V_ROWS, tq), lambda i: (i, 0, 0)),
                   pl.BlockSpec((per, N_GROUPS * V_ROWS, WIN_CHUNK), lambda i: (i, 0, 0)),
                   pl.BlockSpec((ng, tq), lambda i: (0, i))],
        out_shape=[jax.ShapeDtypeStruct((nqr, t), BF16),
                   jax.ShapeDtypeStruct((t // tq, N_GROUPS * V_ROWS, tq), BF16),
                   jax.ShapeDtypeStruct((t // WIN_CHUNK, N_GROUPS * V_ROWS, WIN_CHUNK), BF16),
                   jax.ShapeDtypeStruct((ng, t), F32)],
        compiler_params=_cparams("parallel"),
        name="nt_proj",
    )(h, w_qt, w_vt, w_gt)


def _compress_call(xc, wa, wb, pe8, w2, *, transposed, cc):
    nc, kk = xc.shape

    def body(x_ref, wa_ref, wb_ref, pe_ref, w2_ref, o_ref):
        xv = x_ref[...]
        top = jnp.dot(xv, wa_ref[...], preferred_element_type=F32)
        bot = jnp.dot(xv, wb_ref[...], preferred_element_type=F32)
        pe_top = jnp.dot(pe_ref[...], wa_ref[...], preferred_element_type=F32)
        pe_bot = jnp.dot(pe_ref[...], wb_ref[...], preferred_element_type=F32)
        pre = top + pltpu.roll(bot, nc - 1, 0) + pe_top[0:1] + pe_bot[1:2]
        hid = (pre * _sigmoid(pre)).astype(BF16)
        if transposed:
            r = _nt_dot(w2_ref[...], hid)
            ones_rows = jnp.where(lax.broadcasted_iota(jnp.int32, (V_ROWS - HEAD_DIM, cc), 0) == 0, 1.0, 0.0)
            for c in range(nc // cc):
                o_ref[c] = jnp.concatenate([r[:, c * cc:(c + 1) * cc], ones_rows], axis=0).astype(BF16)
        else:
            o_ref[...] = jnp.dot(hid, w2_ref[...], preferred_element_type=F32).astype(BF16)

    if transposed:
        out_spec = pl.BlockSpec((None, nc // cc, V_ROWS, cc), lambda g: (g, 0, 0, 0))
        out_shape = jax.ShapeDtypeStruct((N_GROUPS, nc // cc, V_ROWS, cc), BF16)
    else:
        out_spec = pl.BlockSpec((None, nc, HEAD_DIM), lambda g: (g, 0, 0))
        out_shape = jax.ShapeDtypeStruct((N_GROUPS, nc, HEAD_DIM), BF16)
    return pl.pallas_call(
        body,
        grid=(N_GROUPS,),
        in_specs=[pl.BlockSpec((nc, kk), lambda g: (0, 0)),
                  pl.BlockSpec((None, kk, CMP_HIDDEN), lambda g: (g, 0, 0)),
                  pl.BlockSpec((None, kk, CMP_HIDDEN), lambda g: (g, 0, 0)),
                  pl.BlockSpec((8, kk), lambda g: (0, 0)),
                  pl.BlockSpec(w2.shape, lambda g: (0, 0))],
        out_specs=out_spec,
        out_shape=out_shape,
        compiler_params=_cparams("parallel"),
        name="compress_v" if transposed else "compress_k",
    )(xc, wa, wb, pe8, w2)


def _nsa_feature_tables(nc):
    def table(nrows, pos_step, coarse, onehots):
        n = np.arange(nrows, dtype=np.int32)
        pos = n * pos_step
        hi = (pos // coarse) * coarse
        lo = pos - hi
        cols = [hi, hi, hi, lo, lo, lo] + [np.zeros_like(n)] * (FEAT_ONEHOT - 6)
        if onehots:
            cols += [(n // SEL_BLOCK == b).astype(np.int32) for b in range(SLC_CHUNK // SEL_BLOCK)]
        cols += [np.zeros_like(n)] * (HEAD_DIM - len(cols))
        return jnp.asarray(np.stack(cols, axis=1).astype(np.float32)).astype(BF16)

    slc = table(SLC_CHUNK, 1, 16, True)
    win = table(WINDOW + Q_TILE, 1, 16, False)
    cmp_ = table(min(CMP_CHUNK, nc), CMP_STRIDE, 256, False)
    return slc, win, cmp_


def _nsa_call(qt, proj, kc, vct, vst, vwt, gates, slopes, qfeat, *, t, ks_col, kw_col):
    nq = t // Q_TILE
    n_sel = t // SEL_BLOCK
    nc = t // CMP_STRIDE
    cc = min(CMP_CHUNK, nc)
    top_n = min(SEL_TOP_N, n_sel)
    hg = HEADS_PER_GROUP
    wide = hg * Q_TILE
    sel_per_chunk = SLC_CHUNK // SEL_BLOCK
    win_span = WINDOW + Q_TILE
    kdim = 2 * HEAD_DIM
    tab_slc, tab_win, tab_cmp = _nsa_feature_tables(nc)

    def body(q_ref, kc_ref, vct_ref, ks_ref, vst_ref, kw_ref, vwt_ref, g_ref, sl_ref, qf_ref,
             ts_ref, tw_ref, tc_ref, o_ref,
             qt_ref, kss_ref, ksw_ref, ksc_ref, m_ref, acc_ref, p_ref, mw_ref, accw_ref, pw_ref,
             out_ref, s_ref, ps_ref, score_ref, sel_ref, flag_ref):
        g = pl.program_id(0)
        i = pl.program_id(1)
        t0 = i * Q_TILE
        tq = t0 + lax.broadcasted_iota(jnp.int32, (1, Q_TILE), 1)

        for h in range(hg):
            qt_ref[0:HEAD_DIM, h * Q_TILE:(h + 1) * Q_TILE] = q_ref[h * HEAD_DIM:(h + 1) * HEAD_DIM, :]
        qt_ref[HEAD_DIM:HEAD_DIM + FEAT_ONEHOT, :] = qf_ref[...]
        qt_ref[HEAD_DIM + FEAT_ONEHOT:kdim, :] = jnp.zeros((HEAD_DIM - FEAT_ONEHOT, wide), BF16)
        kss_ref[:, HEAD_DIM:kdim] = ts_ref[...]
        ksw_ref[:, HEAD_DIM:kdim] = tw_ref[...]
        ksc_ref[:, HEAD_DIM:kdim] = tc_ref[...]

        state_main = (m_ref, acc_ref, p_ref)
        state_win = (mw_ref, accw_ref, pw_ref)

        def reset(state):
            state[0][...] = jnp.full((1, wide), NEG_INF, F32)
            state[1][...] = jnp.zeros((V_ROWS, wide), F32)

        def chunk_offset(ref_pos):
            return sl_ref[...] * (ref_pos - t0).astype(F32)

        def row_blocks(nrows):
            return [(r, min(ROW_BLOCK, nrows - r)) for r in range(0, nrows, ROW_BLOCK)]

        def scores(kst_ref, nrows):
            return [jnp.dot(kst_ref[r:r + n, :], qt_ref[...], preferred_element_type=F32)
                    for r, n in row_blocks(nrows)]

        def softmax_pv(state, sts, vt_c, off, nrows, mask=None):
            m_ref, acc_ref, p_ref = state
            for (r, n), st in zip(row_blocks(nrows), sts):
                alphas = []
                keep = None if mask is None else mask(r, n)
                for h in range(hg):
                    sl = slice(h * Q_TILE, (h + 1) * Q_TILE)
                    s = st[:, sl]
                    if keep is not None:
                        s = jnp.where(keep, s, NEG_INF)
                    o = off[:, sl]
                    m_old = m_ref[:, sl]
                    m_new = jnp.maximum(m_old, jnp.max(s, axis=0, keepdims=True) + o)
                    alphas.append(jnp.exp2(m_old - m_new))
                    m_ref[:, sl] = m_new
                    p_ref[r:r + n, sl] = jnp.exp2(s - (m_new - o)).astype(BF16)
                pv = jnp.dot(vt_c[:, r:r + n], p_ref[r:r + n, :], preferred_element_type=F32)
                acc_ref[...] = acc_ref[...] * jnp.concatenate(alphas, axis=1) + pv

        def attend(state, kst_ref, vt_c, off, nrows, mask=None):
            softmax_pv(state, scores(kst_ref, nrows), vt_c, off, nrows, mask)

        def inv_norm(state, sl):
            valid = state[0][:, sl] > 0.5 * NEG_INF
            denom = state[1][HEAD_DIM:HEAD_DIM + 1, sl]
            return jnp.where(valid, 1.0 / jnp.maximum(denom, 1e-30), 0.0)

        def finish(state, branch, first):
            for h in range(hg):
                sl = slice(h * Q_TILE, (h + 1) * Q_TILE)
                gate = g_ref[pl.ds(branch * N_HEADS + g * hg + h, 1), :]
                contrib = state[1][0:HEAD_DIM, sl] * (inv_norm(state, sl) * gate)
                if first:
                    out_ref[:, sl] = contrib
                else:
                    out_ref[:, sl] += contrib

        reset(state_main)
        ps_ref[...] = jnp.zeros(ps_ref.shape, F32)
        n_cmp_keys = (t0 + Q_TILE - CMP_BLOCK) // CMP_STRIDE + 1
        n_cc = (n_cmp_keys + cc - 1) // cc

        def cmp_scores(c, carry):
            r0 = pl.multiple_of(c * cc, cc)
            n = r0 + lax.broadcasted_iota(jnp.int32, (cc, Q_TILE), 0)
            mask = (n * CMP_STRIDE + (CMP_BLOCK - 1)) <= tq
            off = chunk_offset(r0 * CMP_STRIDE + (CMP_BLOCK - 1))
            ksc_ref[:, 0:HEAD_DIM] = kc_ref[pl.ds(r0, cc), :]
            for hp in range(hg // 2):
                pair = slice(hp * 2 * Q_TILE, (hp + 1) * 2 * Q_TILE)
                st = jnp.dot(ksc_ref[...], qt_ref[:, pair], preferred_element_type=F32)
                for hh in range(2):
                    sl = slice((2 * hp + hh) * Q_TILE, (2 * hp + hh + 1) * Q_TILE)
                    s = jnp.where(mask, st[:, hh * Q_TILE:(hh + 1) * Q_TILE] + off[:, sl], NEG_INF)
                    s_ref[pl.ds(r0, cc), sl] = s
                    m_ref[:, sl] = jnp.maximum(m_ref[:, sl], jnp.max(s, axis=0, keepdims=True))
            return carry

        lax.fori_loop(0, n_cc, cmp_scores, 0)

        def cmp_probs(c, carry):
            r0 = pl.multiple_of(c * cc, cc)
            for hp in range(hg // 2):
                pair = slice(hp * 2 * Q_TILE, (hp + 1) * 2 * Q_TILE)
                for hh in range(2):
                    sl = slice((2 * hp + hh) * Q_TILE, (2 * hp + hh + 1) * Q_TILE)
                    p = jnp.exp2(s_ref[pl.ds(r0, cc), sl] - m_ref[:, sl])
                    s_ref[pl.ds(r0, cc), sl] = p
                    p_ref[0:cc, sl] = p.astype(BF16)
                acc_ref[:, pair] += jnp.dot(vct_ref[c], p_ref[0:cc, pair], preferred_element_type=F32)
            return carry

        lax.fori_loop(0, n_cc, cmp_probs, 0)
        finish(state_main, 0, True)

        def ps_body(c, carry):
            r0 = pl.multiple_of(c * cc, cc)
            tot = jnp.zeros((cc, Q_TILE), F32)
            for h in range(hg):
                sl = slice(h * Q_TILE, (h + 1) * Q_TILE)
                tot = tot + s_ref[pl.ds(r0, cc), sl] * inv_norm(state_main, sl)
            ps_ref[pl.ds(8 + r0, cc), :] = tot
            return carry

        lax.fori_loop(0, n_cc, ps_body, 0)

        ratio = SEL_BLOCK // CMP_STRIDE
        imp = ps_ref[pl.ds(7, n_sel, stride=ratio), :]
        for kk in range(ratio):
            imp = imp + ps_ref[pl.ds(8 + kk, n_sel, stride=ratio), :]
        jrow = lax.broadcasted_iota(jnp.int32, (n_sel, Q_TILE), 0)
        cur = tq // SEL_BLOCK
        forced = jnp.where(jrow == 0, 1.0, jnp.where(jrow == cur, 1.0, jnp.where(jrow == cur - 1, 1.0, 0.0)))
        score_ref[...] = jnp.where(forced > 0.5, -2.0, jnp.where(jrow <= cur, imp, -1.0))
        sel_ref[...] = forced
        jrow_f = jrow.astype(F32)

        def run_topk(nrows):
            jr = jrow_f[0:nrows, :]

            def topk_body(r, carry):
                sc = score_ref[0:nrows, :]
                mx = jnp.max(sc, axis=0, keepdims=True)
                first = jnp.min(jnp.where(sc == mx, jr, float(n_sel)), axis=0, keepdims=True)
                pick = jr == first
                sel_ref[0:nrows, :] = jnp.where(pick, 1.0, sel_ref[0:nrows, :])
                score_ref[0:nrows, :] = jnp.where(pick, -2.0, sc)
                return carry

            lax.fori_loop(0, top_n - N_FORCED, topk_body, 0)

        half_rows = n_sel // 2
        tile_in_first_half = (t0 + Q_TILE - 1) // SEL_BLOCK < half_rows

        @pl.when(tile_in_first_half)
        def _():
            run_topk(half_rows)

        @pl.when(jnp.logical_not(tile_in_first_half))
        def _():
            run_topk(n_sel)
        for c in range(t // SLC_CHUNK):
            blocks = sel_ref[c * sel_per_chunk:(c + 1) * sel_per_chunk, :]
            flag_ref[c] = (jnp.max(blocks) > 0.5).astype(jnp.int32)

        reset(state_main)
        last_sc = (t0 + Q_TILE - 1) // SLC_CHUNK

        def slc_stage(c):
            r0 = pl.multiple_of(c * SLC_CHUNK, SLC_CHUNK)
            kss_ref[:, 0:HEAD_DIM] = ks_ref[pl.ds(r0, SLC_CHUNK), :]
            sel8 = sel_ref[pl.ds(pl.multiple_of(c * sel_per_chunk, sel_per_chunk), sel_per_chunk), :]
            bias = (sel8 - 1.0) * MASK_BIG
            rows = jnp.concatenate([jnp.concatenate([bias] * hg, axis=1),
                                    jnp.zeros((FEAT_ONEHOT - sel_per_chunk, wide), F32)], axis=0)
            qt_ref[HEAD_DIM + FEAT_ONEHOT:HEAD_DIM + 2 * FEAT_ONEHOT, :] = rows.astype(BF16)
            return r0

        def slc_body(c, carry):
            @pl.when(flag_ref[c] > 0)
            def _():
                r0 = slc_stage(c)
                attend(state_main, kss_ref, vst_ref[c], chunk_offset(r0), SLC_CHUNK)

            return carry

        lax.fori_loop(0, last_sc, slc_body, 0)

        reset(state_win)
        r0s = slc_stage(last_sc)
        n_wc = win_span // WIN_CHUNK
        c0 = jnp.maximum(i * (Q_TILE // WIN_CHUNK) - WINDOW // WIN_CHUNK, 0)
        r0w = pl.multiple_of(c0 * WIN_CHUNK, WIN_CHUNK)
        ksw_ref[:, 0:HEAD_DIM] = kw_ref[pl.ds(r0w, win_span), :]
        sts_slc = scores(kss_ref, SLC_CHUNK)
        sts_win = scores(ksw_ref, win_span)

        def masks(r, n):
            return (r0s + r + lax.broadcasted_iota(jnp.int32, (n, Q_TILE), 0)) <= tq

        softmax_pv(state_main, sts_slc, vst_ref[last_sc], chunk_offset(r0s), SLC_CHUNK, masks)
        finish(state_main, 1, False)

        def maskw(r, n):
            dpos = (r0w + r + lax.broadcasted_iota(jnp.int32, (n, Q_TILE), 0)) - tq
            return jnp.where(dpos <= 0, dpos, -2 * WINDOW) > -WINDOW

        vtw = jnp.concatenate([vwt_ref[c0 + kk] for kk in range(n_wc)], axis=1)
        softmax_pv(state_win, sts_win, vtw, chunk_offset(r0w), win_span, maskw)
        finish(state_win, 2, False)

        for h in range(hg):
            sl = slice(h * Q_TILE, (h + 1) * Q_TILE)
            o_ref[:, h * HEAD_DIM:(h + 1) * HEAD_DIM] = out_ref[:, sl].T.astype(o_ref.dtype)

    one = pl.Buffered(1)
    in_specs = [
        pl.BlockSpec((hg * HEAD_DIM, Q_TILE), lambda g, i: (g, i)),
        pl.BlockSpec((None, nc, HEAD_DIM), lambda g, i: (g, 0, 0)),
        pl.BlockSpec((None, nc // cc, V_ROWS, cc), lambda g, i: (g, 0, 0, 0)),
        pl.BlockSpec((t, HEAD_DIM), lambda g, i: (0, ks_col + g), pipeline_mode=one),
        pl.BlockSpec((t // SLC_CHUNK, V_ROWS, SLC_CHUNK), lambda g, i: (0, g, 0), pipeline_mode=one),
        pl.BlockSpec((t, HEAD_DIM), lambda g, i: (0, kw_col + g), pipeline_mode=one),
        pl.BlockSpec((t // WIN_CHUNK, V_ROWS, WIN_CHUNK), lambda g, i: (0, g, 0), pipeline_mode=one),
        pl.BlockSpec((gates.shape[0], Q_TILE), lambda g, i: (0, i)),
        pl.BlockSpec((None, 1, wide), lambda g, i: (g, 0, 0)),
        pl.BlockSpec((None, FEAT_ONEHOT, wide), lambda g, i: (g, 0, 0)),
        pl.BlockSpec(tab_slc.shape, lambda g, i: (0, 0)),
        pl.BlockSpec(tab_win.shape, lambda g, i: (0, 0)),
        pl.BlockSpec(tab_cmp.shape, lambda g, i: (0, 0)),
    ]
    scratch = [
        pltpu.VMEM((kdim, wide), BF16),
        pltpu.VMEM((SLC_CHUNK, kdim), BF16),
        pltpu.VMEM((win_span, kdim), BF16),
        pltpu.VMEM((cc, kdim), BF16),
        pltpu.VMEM((1, wide), F32),
        pltpu.VMEM((V_ROWS, wide), F32),
        pltpu.VMEM((SLC_CHUNK, wide), BF16),
        pltpu.VMEM((1, wide), F32),
        pltpu.VMEM((V_ROWS, wide), F32),
        pltpu.VMEM((win_span, wide), BF16),
        pltpu.VMEM((HEAD_DIM, wide), F32),
        pltpu.VMEM((nc, wide), F32),
        pltpu.VMEM((8 + nc, Q_TILE), F32),
        pltpu.VMEM((n_sel, Q_TILE), F32),
        pltpu.VMEM((n_sel, Q_TILE), F32),
        pltpu.SMEM((t // SLC_CHUNK,), jnp.int32),
    ]
    return pl.pallas_call(
        body,
        grid=(N_GROUPS, nq),
        in_specs=in_specs,
        out_specs=pl.BlockSpec((Q_TILE, hg * HEAD_DIM), lambda g, i: (i, g)),
        out_shape=jax.ShapeDtypeStruct((t, N_HEADS * HEAD_DIM), BF16),
        scratch_shapes=scratch,
        compiler_params=_cparams("parallel", "parallel"),
        name="nsa",
    )(qt, kc, vct, proj, vst, proj, vwt, gates, slopes, qfeat, tab_slc, tab_win, tab_cmp)


def _glu_proj_call(h, w_glu):
    t, d = h.shape
    ch = w_glu.shape[1] // 2
    tm = min(1024, t)
    tn = 512
    nj = ch // tn

    def body(h_ref, wa_ref, wg_ref, o_ref):
        hv = h_ref[...]
        a = jnp.dot(hv, wa_ref[...], preferred_element_type=F32)
        g = jnp.dot(hv, wg_ref[...], preferred_element_type=F32)
        o_ref[...] = (a * _sigmoid(g)).astype(o_ref.dtype)

    return pl.pallas_call(
        body,
        grid=(t // tm, nj),
        in_specs=[pl.BlockSpec((tm, d), lambda i, j: (i, 0)),
                  pl.BlockSpec((d, tn), lambda i, j: (0, j)),
                  pl.BlockSpec((d, tn), lambda i, j: (0, nj + j))],
        out_specs=pl.BlockSpec((tm, tn), lambda i, j: (i, j)),
        out_shape=jax.ShapeDtypeStruct((t, ch), BF16),
        compiler_params=_cparams("parallel", "parallel"),
        name="proj_glu",
    )(h, w_glu, w_glu)


def _conv_front_call(u, w_dw, b_dw, ln_g, ln_b):
    t = u.shape[0]
    ch = w_dw.shape[1]
    tm = min(256, t)
    halo = 32
    rb = 32
    strip = 512
    w_rep = jnp.repeat(w_dw, F32_SUBLANES, axis=0)

    def body(a_ref, ah_ref, w_ref, b_ref, lg_ref, lb_ref, o_ref, u_ref, sb_ref, y_ref):
        i = pl.program_id(0)
        u_ref[halo:halo + tm, :] = a_ref[...].astype(F32)
        u_ref[0:halo, :] = jnp.where(i > 0, ah_ref[...].astype(F32), 0.0)
        for cs in range(ch // strip):
            cols = slice(cs * strip, (cs + 1) * strip)
            ue = u_ref[:, cols]
            sb_ref[0] = ue
            for part in range(1, F32_SUBLANES):
                sb_ref[part] = pltpu.roll(ue, part, 0)

            def rows(r, carry, cols=cols):
                r0 = pl.multiple_of(r * rb, rb)
                acc = jnp.zeros((rb // F32_SUBLANES, F32_SUBLANES, strip), F32)
                for s in range(CONV_WIDTH):
                    whole, part = divmod(s, F32_SUBLANES)
                    k = CONV_WIDTH - 1 - s
                    tap = sb_ref[part, pl.ds(halo + r0 - F32_SUBLANES * whole, rb), :]
                    wk = w_ref[k * F32_SUBLANES:(k + 1) * F32_SUBLANES, cols]
                    acc = acc + tap.reshape(rb // F32_SUBLANES, F32_SUBLANES, strip) * wk
                y_ref[pl.ds(r0, rb), cols] = acc.reshape(rb, strip) + b_ref[:, cols]
                return carry

            lax.fori_loop(0, tm // rb, rows, 0)
        y = y_ref[...]
        mu = jnp.mean(y, axis=-1, keepdims=True)
        yc = y - mu
        var = jnp.mean(yc * yc, axis=-1, keepdims=True)
        z = yc * lax.rsqrt(var + EPS) * lg_ref[...] + lb_ref[...]
        o_ref[...] = (z * _sigmoid(z)).astype(o_ref.dtype)

    per = tm // halo
    return pl.pallas_call(
        body,
        grid=(t // tm,),
        in_specs=[pl.BlockSpec((tm, ch), lambda i: (i, 0)),
                  pl.BlockSpec((halo, ch), lambda i: (jnp.maximum(i * per - 1, 0), 0)),
                  pl.BlockSpec((CONV_WIDTH * F32_SUBLANES, ch), lambda i: (0, 0)),
                  pl.BlockSpec((1, ch), lambda i: (0, 0)),
                  pl.BlockSpec((1, ch), lambda i: (0, 0)),
                  pl.BlockSpec((1, ch), lambda i: (0, 0))],
        out_specs=pl.BlockSpec((tm, ch), lambda i: (i, 0)),
        out_shape=jax.ShapeDtypeStruct((t, ch), BF16),
        scratch_shapes=[pltpu.VMEM((halo + tm, ch), F32),
                        pltpu.VMEM((F32_SUBLANES, halo + tm, strip), F32),
                        pltpu.VMEM((tm, ch), F32)],
        compiler_params=_cparams("parallel"),
        name="conv_front",
    )(u, u, w_rep, b_dw.reshape(1, ch), ln_g.reshape(1, ch), ln_b.reshape(1, ch))


def _ffn_up_call(h2, w_up, w_dw, b_dw):
    t, d = h2.shape
    n2 = w_up.shape[1]
    f = n2 // 2
    tm = min(1024, t)
    rsub = 256
    tn = 512
    nj = f // tn
    sub = F32_SUBLANES
    w_pad = jnp.concatenate([w_dw, jnp.zeros((sub - FFN_CONV_WIDTH, n2), F32)], axis=0)
    b2 = b_dw.reshape(1, n2)

    def body(h_ref, wa_ref, wg_ref, da_ref, dg_ref, ba_ref, bg_ref, o_ref, ca_ref, cg_ref):
        i = pl.program_id(0)
        j = pl.program_id(1)

        @pl.when(i == 0)
        def _():
            ca_ref[j] = jnp.zeros((sub, tn), F32)
            cg_ref[j] = jnp.zeros((sub, tn), F32)

        wa = wa_ref[...]
        wg = wg_ref[...]

        def conv(up, prev, d_ref, b_ref):
            ext = jnp.concatenate([prev, up], axis=0)
            acc = b_ref[...] + up * d_ref[FFN_CONV_WIDTH - 1:FFN_CONV_WIDTH, :]
            for s in range(1, FFN_CONV_WIDTH):
                k = FFN_CONV_WIDTH - 1 - s
                acc = acc + pltpu.roll(ext, s, 0)[sub:, :] * d_ref[k:k + 1, :]
            return acc

        def project(rs):
            hv = h_ref[rs * rsub:(rs + 1) * rsub, :]
            return (jnp.dot(hv, wa, preferred_element_type=F32),
                    jnp.dot(hv, wg, preferred_element_type=F32))

        prev_a = ca_ref[j]
        prev_g = cg_ref[j]
        n_rs = tm // rsub
        ups = project(0)
        for rs in range(n_rs):
            rows = slice(rs * rsub, (rs + 1) * rsub)
            up_a, up_g = ups
            if rs + 1 < n_rs:
                ups = project(rs + 1)
            ca = conv(up_a, prev_a, da_ref, ba_ref)
            cg = conv(up_g, prev_g, dg_ref, bg_ref)
            o_ref[rows, :] = (cg * _sigmoid(cg) * ca).astype(o_ref.dtype)
            prev_a = up_a[rsub - sub:, :]
            prev_g = up_g[rsub - sub:, :]
        ca_ref[j] = prev_a
        cg_ref[j] = prev_g

    return pl.pallas_call(
        body,
        grid=(t // tm, nj),
        in_specs=[pl.BlockSpec((tm, d), lambda i, j: (i, 0)),
                  pl.BlockSpec((d, tn), lambda i, j: (0, j)),
                  pl.BlockSpec((d, tn), lambda i, j: (0, nj + j)),
                  pl.BlockSpec((sub, tn), lambda i, j: (0, j)),
                  pl.BlockSpec((sub, tn), lambda i, j: (0, nj + j)),
                  pl.BlockSpec((1, tn), lambda i, j: (0, j)),
                  pl.BlockSpec((1, tn), lambda i, j: (0, nj + j))],
        out_specs=pl.BlockSpec((tm, tn), lambda i, j: (i, j)),
        out_shape=jax.ShapeDtypeStruct((t, f), BF16),
        scratch_shapes=[pltpu.VMEM((nj, sub, tn), F32), pltpu.VMEM((nj, sub, tn), F32)],
        compiler_params=_cparams("arbitrary", "arbitrary"),
        name="ffn_up",
    )(h2, w_up, w_up, w_pad, w_pad, b2, b2)


def _layer(x, c_row, p):
    t, d = x.shape
    ada = _ada_call(c_row, p["w_ada"], p["b_ada"])

    w_in = p["w_in"]
    qw = N_HEADS * HEAD_DIM
    kvw = N_GROUPS * HEAD_DIM
    o = [0, qw]
    for _ in range(6):
        o.append(o[-1] + kvw)
    o.append(o[-1] + 3 * N_HEADS)
    o.append(o[-1] + 2 * d)
    o.append(o[-1] + 2 * d)
    w_q, w_kc, w_vc, w_ks, w_vs, w_kw, w_vw, w_gn, w_glu, w_mg = [w_in[:, o[k]:o[k + 1]] for k in range(10)]
    w_rm = jnp.concatenate([w_kc, w_vc, w_ks, w_kw, w_mg], axis=1).astype(BF16)
    w_glu = w_glu.astype(BF16)
    w_t = _transpose_cast_call(w_in, [(o[0], qw), (o[4], kvw), (o[6], kvw)])
    w_qt = w_t[0:qw]
    w_vt = w_t[qw:qw + 2 * kvw]
    n_gate_rows = 64
    w_gt = jnp.concatenate([w_gn.T, jnp.zeros((n_gate_rows - 3 * N_HEADS, d), F32)], axis=0).astype(BF16)

    h1 = _norm_call(x, p["norm1_g"], ada, shift_idx=0, scale_idx=1)
    proj = _matmul_call(h1, w_rm, tm=1024, tn=1024, out_dtype=BF16, name="proj_rm")
    u = _glu_proj_call(h1, w_glu)
    qt, vst, vwt, gates = _nt_proj_call(h1, w_qt, w_vt, w_gt, HEAD_DIM ** -0.5 * LOG2E)

    nc = t // CMP_STRIDE
    cc = min(CMP_CHUNK, nc)
    half = CMP_BLOCK // CMP_STRIDE
    assert half == 2

    def expand(w1):
        w1r = w1.reshape(half, CMP_STRIDE, HEAD_DIM, CMP_HIDDEN)
        outs = []
        for hf in range(half):
            per_g = []
            for g in range(N_GROUPS):
                z = jnp.zeros((CMP_STRIDE, N_GROUPS, HEAD_DIM, CMP_HIDDEN), F32).at[:, g].set(w1r[hf])
                per_g.append(z.reshape(CMP_STRIDE * kvw, CMP_HIDDEN))
            outs.append(jnp.stack(per_g).astype(BF16))
        return outs

    pe = p["cmp_pe"].reshape(half, CMP_STRIDE, 1, HEAD_DIM)
    pe_rows = jnp.broadcast_to(pe, (half, CMP_STRIDE, N_GROUPS, HEAD_DIM)).reshape(half, CMP_STRIDE * kvw)
    pe8 = jnp.concatenate([pe_rows, jnp.zeros((8 - half, CMP_STRIDE * kvw), F32)], axis=0).astype(BF16)
    kc_raw = proj[:, 0:kvw].reshape(nc, CMP_STRIDE * kvw)
    vc_raw = proj[:, kvw:2 * kvw].reshape(nc, CMP_STRIDE * kvw)
    wka, wkb = expand(p["w_kc1"])
    wva, wvb = expand(p["w_vc1"])
    kc = _compress_call(kc_raw, wka, wkb, pe8, p["w_kc2"].astype(BF16), transposed=False, cc=cc)
    vct = _compress_call(vc_raw, wva, wvb, pe8, p["w_vc2"].T.astype(BF16), transposed=True, cc=cc)

    head_ids = np.arange(1, N_HEADS + 1, dtype=np.float64)
    slopes_np = (np.exp2(-8.0 * head_ids / N_HEADS) * LOG2E).astype(np.float32)
    slopes_np = np.broadcast_to(slopes_np.reshape(N_GROUPS, 1, HEADS_PER_GROUP, 1),
                                (N_GROUPS, 1, HEADS_PER_GROUP, Q_TILE)).reshape(N_GROUPS, 1, HEADS_PER_GROUP * Q_TILE)
    s1 = slopes_np.astype(BF16)
    s2 = (slopes_np - s1.astype(np.float32)).astype(BF16)
    s3 = (slopes_np - s1.astype(np.float32) - s2.astype(np.float32)).astype(BF16)
    zero_rows = np.zeros((N_GROUPS, FEAT_ONEHOT - 6, slopes_np.shape[-1]), dtype=s1.dtype)
    qfeat = jnp.asarray(np.concatenate([s1, s2, s3, s1, s2, s3, zero_rows], axis=1))
    slopes = jnp.asarray(slopes_np)
    o_nsa = _nsa_call(qt, proj, kc, vct, vst, vwt, gates, slopes, qfeat, t=t,
                      ks_col=(2 * kvw) // HEAD_DIM, kw_col=(3 * kvw) // HEAD_DIM)

    u2 = _conv_front_call(u, p["conv_dw_w"], p["conv_dw_b"], p["conv_ln_g"], p["conv_ln_b"])

    merged = _merge_call(o_nsa, p["w_o_nsa"].astype(BF16), u2, p["conv_pw_w"].astype(BF16), p["conv_pw_b"],
                         proj, gate_col=4 * kvw)
    x1, h2 = _out_norm_call(merged, p["w_out"].astype(BF16), x, ada, p["norm2_g"],
                            gate_idx=2, shift_idx=3, scale_idx=4)
    act = _ffn_up_call(h2, p["ffn_w_up"].astype(BF16), p["ffn_dw_w"], p["ffn_dw_b"])
    x2 = _matmul_call(
        act, p["ffn_w_down"].astype(BF16), tm=512, tn=512, out_dtype=F32, name="ffn_down",
        epilogue=lambda acc, x_ref, g_ref: x_ref[...] + g_ref[0:1, :] * acc,
        extras=(x1, ada),
        extra_specs=(pl.BlockSpec((512, 512), lambda i, j: (i, j)),
                     pl.BlockSpec((8, 512), lambda i, j: (0, 5 * (d // 512) + j))))
    return x2


def kernel(x, c, w_ada, b_ada, norm1_g, w_in, cmp_pe, w_kc1, w_kc2, w_vc1, w_vc2, w_o_nsa, conv_dw_w, conv_dw_b, conv_ln_g, conv_ln_b, conv_pw_w, conv_pw_b, w_out, norm2_g, ffn_w_up, ffn_dw_w, ffn_dw_b, ffn_w_down, final_g):
    b, t, d = x.shape
    stacked = dict(w_ada=w_ada, b_ada=b_ada, norm1_g=norm1_g, w_in=w_in, cmp_pe=cmp_pe, w_kc1=w_kc1,
                   w_kc2=w_kc2, w_vc1=w_vc1, w_vc2=w_vc2, w_o_nsa=w_o_nsa, conv_dw_w=conv_dw_w,
                   conv_dw_b=conv_dw_b, conv_ln_g=conv_ln_g, conv_ln_b=conv_ln_b, conv_pw_w=conv_pw_w,
                   conv_pw_b=conv_pw_b, w_out=w_out, norm2_g=norm2_g, ffn_w_up=ffn_w_up,
                   ffn_dw_w=ffn_dw_w, ffn_dw_b=ffn_dw_b, ffn_w_down=ffn_w_down)
    depth = w_ada.shape[0]
    xs = x.reshape(b * t, d)
    outs = []
    for bi in range(b):
        xb = xs if b == 1 else xs[bi * t:(bi + 1) * t]
        for layer in range(depth):
            xb = _layer(xb, c[bi:bi + 1], {k: v[layer] for k, v in stacked.items()})
        outs.append(_norm_call(xb, final_g, out_dtype=x.dtype))
    out = outs[0] if b == 1 else jnp.concatenate(outs, axis=0)
    return out.reshape(b, t, d)
```

```python
import functools

import jax
import jax.numpy as jnp
import numpy as np
from jax import lax
from jax.experimental import pallas as pl
from jax.experimental.pallas import tpu as pltpu

F32 = jnp.float32
BF16 = jnp.bfloat16

N_HEADS = 16
N_GROUPS = 2
HEADS_PER_GROUP = N_HEADS // N_GROUPS
HEAD_DIM = 128
CMP_BLOCK = 32
CMP_STRIDE = 16
CMP_HIDDEN = 2 * HEAD_DIM
SEL_BLOCK = 64
SEL_TOP_N = 16
WINDOW = 512
FORCE_BONUS = 1e6
CONV_WIDTH = 31
FFN_CONV_WIDTH = 3
EPS = 1e-6
NEG_INF = -1e30
LOG2E = 1.4426950408889634

V7X_VMEM_BYTES = 64 * 1024 * 1024
VMEM_LIMIT_BYTES = V7X_VMEM_BYTES - 8 * 1024 * 1024
LANES = 128
F32_SUBLANES = 8

Q_TILE = 128
SLC_CHUNK = 512
WIN_CHUNK = 128
CMP_CHUNK = 256
FEAT_ONEHOT = 16
MASK_BIG = 2.0 ** 20
N_FORCED = 3
ROW_BLOCK = 256
V_ROWS = HEAD_DIM + 16


def _cparams(*sem):
    return pltpu.CompilerParams(dimension_semantics=sem, vmem_limit_bytes=VMEM_LIMIT_BYTES)


def _sigmoid(v):
    return 1.0 / (1.0 + jnp.exp(-v))


def _nt_dot(a, b):
    return lax.dot_general(a, b, (((1,), (1,)), ((), ())), preferred_element_type=F32)


def _ada_call(c_row, w, b):
    d, n = w.shape
    tn = 1024
    cb = jnp.broadcast_to(c_row.reshape(d, 1), (d, LANES))

    def body(cb_ref, w_ref, b_ref, o_ref):
        cv = cb_ref[...]
        act = cv * _sigmoid(cv)
        for j in range(tn // LANES):
            sl = slice(j * LANES, (j + 1) * LANES)
            row = jnp.sum(w_ref[:, sl] * act, axis=0, keepdims=True) + b_ref[:, sl]
            o_ref[:, sl] = jnp.broadcast_to(row, (8, LANES))

    return pl.pallas_call(
        body,
        grid=(n // tn,),
        in_specs=[pl.BlockSpec((d, LANES), lambda j: (0, 0)),
                  pl.BlockSpec((d, tn), lambda j: (0, j)),
                  pl.BlockSpec((1, tn), lambda j: (0, j))],
        out_specs=pl.BlockSpec((8, tn), lambda j: (0, j)),
        out_shape=jax.ShapeDtypeStruct((8, n), F32),
        compiler_params=_cparams("parallel"),
        name="ada",
    )(cb, w, b.reshape(1, n))


def _norm_call(x, gain, ada=None, shift_idx=0, scale_idx=0, out_dtype=BF16):
    t, d = x.shape
    tm = 512

    def body(*refs):
        if ada is None:
            x_ref, g_ref, o_ref = refs
        else:
            x_ref, g_ref, sh_ref, sc_ref, o_ref = refs
        xv = x_ref[...]
        y = xv * lax.rsqrt(jnp.mean(xv * xv, axis=-1, keepdims=True) + EPS) * g_ref[...]
        if ada is not None:
            y = y * (1.0 + sc_ref[0:1, :]) + sh_ref[0:1, :]
        o_ref[...] = y.astype(o_ref.dtype)

    in_specs = [pl.BlockSpec((tm, d), lambda i: (i, 0)), pl.BlockSpec((1, d), lambda i: (0, 0))]
    args = [x, gain.reshape(1, d)]
    if ada is not None:
        in_specs += [pl.BlockSpec((8, d), lambda i: (0, shift_idx)),
                     pl.BlockSpec((8, d), lambda i: (0, scale_idx))]
        args += [ada, ada]
    return pl.pallas_call(
        body,
        grid=(t // tm,),
        in_specs=in_specs,
        out_specs=pl.BlockSpec((tm, d), lambda i: (i, 0)),
        out_shape=jax.ShapeDtypeStruct((t, d), out_dtype),
        compiler_params=_cparams("parallel"),
        name="norm",
    )(*args)


def _matmul_call(a, w, *, tm, tn, out_dtype, name, epilogue=None, extras=(), extra_specs=()):
    m, k = a.shape
    n = w.shape[1]
    tm = min(tm, m)

    def body(a_ref, w_ref, *rest):
        o_ref = rest[-1]
        acc = jnp.dot(a_ref[...], w_ref[...].astype(BF16), preferred_element_type=F32)
        if epilogue is not None:
            acc = epilogue(acc, *rest[:-1])
        o_ref[...] = acc.astype(o_ref.dtype)

    return pl.pallas_call(
        body,
        grid=(m // tm, n // tn),
        in_specs=[pl.BlockSpec((tm, k), lambda i, j: (i, 0)),
                  pl.BlockSpec((k, tn), lambda i, j: (0, j))] + list(extra_specs),
        out_specs=pl.BlockSpec((tm, tn), lambda i, j: (i, j)),
        out_shape=jax.ShapeDtypeStruct((m, n), out_dtype),
        compiler_params=_cparams("parallel", "parallel"),
        name=name,
    )(a, w, *extras)


def _transpose_cast_call(w, col_ranges):
    k = w.shape[0]
    tc = 256
    blocks = []
    for start, width in col_ranges:
        assert start % tc == 0 and width % tc == 0, (start, width)
        blocks += [start // tc + b for b in range(width // tc)]
    n_blocks = len(blocks)

    def col_block(i):
        idx = jnp.int32(blocks[0])
        for pos in range(1, n_blocks):
            idx = jnp.where(i >= pos, jnp.int32(blocks[pos]), idx)
        return idx

    def body(w_ref, o_ref):
        o_ref[...] = w_ref[...].T.astype(o_ref.dtype)

    return pl.pallas_call(
        body,
        grid=(n_blocks,),
        in_specs=[pl.BlockSpec((k, tc), lambda i: (0, col_block(i)))],
        out_specs=pl.BlockSpec((tc, k), lambda i: (i, 0)),
        out_shape=jax.ShapeDtypeStruct((n_blocks * tc, k), BF16),
        compiler_params=_cparams("parallel"),
        name="transpose_cast",
    )(w)


def _merge_call(a, wa, b, wb, bias, gates, *, gate_col):
    t, ka = a.shape
    kb = b.shape[1]
    d = wa.shape[1]
    tm = min(512, t)
    tn = 1024
    nb = d // tn
    g0 = gate_col // tn

    def body(a_ref, wa_ref, b_ref, wb_ref, bias_ref, ga_ref, gb_ref, o_ref):
        ya = jnp.dot(a_ref[...], wa_ref[...], preferred_element_type=F32)
        yb = jnp.dot(b_ref[...], wb_ref[...], preferred_element_type=F32) + bias_ref[...]
        o_ref[...] = (_sigmoid(ga_ref[...].astype(F32)) * ya
                      + _sigmoid(gb_ref[...].astype(F32)) * yb).astype(o_ref.dtype)

    return pl.pallas_call(
        body,
        grid=(t // tm, nb),
        in_specs=[pl.BlockSpec((tm, ka), lambda i, j: (i, 0)),
                  pl.BlockSpec((ka, tn), lambda i, j: (0, j)),
                  pl.BlockSpec((tm, kb), lambda i, j: (i, 0)),
                  pl.BlockSpec((kb, tn), lambda i, j: (0, j)),
                  pl.BlockSpec((1, tn), lambda i, j: (0, j)),
                  pl.BlockSpec((tm, tn), lambda i, j: (i, g0 + j)),
                  pl.BlockSpec((tm, tn), lambda i, j: (i, g0 + nb + j))],
        out_specs=pl.BlockSpec((tm, tn), lambda i, j: (i, j)),
        out_shape=jax.ShapeDtypeStruct((t, d), BF16),
        compiler_params=_cparams("parallel", "parallel"),
        name="merge",
    )(a, wa, b, wb, bias.reshape(1, d), gates, gates)


def _out_norm_call(a, w, x, ada, gain, *, gate_idx, shift_idx, scale_idx):
    t, d = x.shape
    tm = min(512, t)

    def body(a_ref, w_ref, x_ref, g_ref, sh_ref, sc_ref, n_ref, x1_ref, h_ref):
        acc = jnp.dot(a_ref[...], w_ref[...], preferred_element_type=F32)
        x1 = x_ref[...] + g_ref[0:1, :] * acc
        x1_ref[...] = x1
        y = x1 * lax.rsqrt(jnp.mean(x1 * x1, axis=-1, keepdims=True) + EPS) * n_ref[...]
        h_ref[...] = (y * (1.0 + sc_ref[0:1, :]) + sh_ref[0:1, :]).astype(h_ref.dtype)

    return pl.pallas_call(
        body,
        grid=(t // tm,),
        in_specs=[pl.BlockSpec((tm, a.shape[1]), lambda i: (i, 0)),
                  pl.BlockSpec(w.shape, lambda i: (0, 0), pipeline_mode=pl.Buffered(1)),
                  pl.BlockSpec((tm, d), lambda i: (i, 0)),
                  pl.BlockSpec((8, d), lambda i: (0, gate_idx)),
                  pl.BlockSpec((8, d), lambda i: (0, shift_idx)),
                  pl.BlockSpec((8, d), lambda i: (0, scale_idx)),
                  pl.BlockSpec((1, d), lambda i: (0, 0))],
        out_specs=[pl.BlockSpec((tm, d), lambda i: (i, 0)),
                   pl.BlockSpec((tm, d), lambda i: (i, 0))],
        out_shape=[jax.ShapeDtypeStruct((t, d), F32), jax.ShapeDtypeStruct((t, d), BF16)],
        compiler_params=_cparams("parallel"),
        name="out_proj_norm",
    )(a, w, x, ada, ada, ada, gain.reshape(1, d))


def _nt_proj_call(h, w_qt, w_vt, w_gt, q_scale):
    t, d = h.shape
    tq = SLC_CHUNK
    nqr = w_qt.shape[0]
    nv = w_vt.shape[0] // 2
    ng = w_gt.shape[0]
    per = tq // WIN_CHUNK

    def body(h_ref, wq_ref, wv_ref, wg_ref, q_ref, vs_ref, vw_ref, g_ref):
        hv = h_ref[...]
        q_ref[...] = (_nt_dot(wq_ref[...], hv) * q_scale).astype(BF16)
        r = _nt_dot(wv_ref[...], hv)
        ones_rows = jnp.where(lax.broadcasted_iota(jnp.int32, (V_ROWS - HEAD_DIM, tq), 0) == 0, 1.0, 0.0)
        for g in range(N_GROUPS):
            full_s = jnp.concatenate([r[g * HEAD_DIM:(g + 1) * HEAD_DIM], ones_rows], axis=0).astype(BF16)
            vs_ref[0, g * V_ROWS:(g + 1) * V_ROWS, :] = full_s
            full_w = jnp.concatenate([r[nv + g * HEAD_DIM:nv + (g + 1) * HEAD_DIM], ones_rows], axis=0).astype(BF16)
            for kk in range(per):
                vw_ref[kk, g * V_ROWS:(g + 1) * V_ROWS, :] = full_w[:, kk * WIN_CHUNK:(kk + 1) * WIN_CHUNK]
        g_ref[...] = _sigmoid(_nt_dot(wg_ref[...], hv))

    return pl.pallas_call(
        body,
        grid=(t // tq,),
        in_specs=[pl.BlockSpec((tq, d), lambda i: (i, 0)),
                  pl.BlockSpec((nqr, d), lambda i: (0, 0)),
                  pl.BlockSpec((2 * nv, d), lambda i: (0, 0)),
                  pl.BlockSpec((ng, d), lambda i: (0, 0))],
        out_specs=[pl.BlockSpec((nqr, tq), lambda i: (0, i)),
                   pl.BlockSpec((1, N_GROUPS * V_ROWS, tq), lambda i: (i, 0, 0)),
                   pl.BlockSpec((per, N_GROUPS * V_ROWS, WIN_CHUNK), lambda i: (i, 0, 0)),
                   pl.BlockSpec((ng, tq), lambda i: (0, i))],
        out_shape=[jax.ShapeDtypeStruct((nqr, t), BF16),
                   jax.ShapeDtypeStruct((t // tq, N_GROUPS * V_ROWS, tq), BF16),
                   jax.ShapeDtypeStruct((t // WIN_CHUNK, N_GROUPS * V_ROWS, WIN_CHUNK), BF16),
                   jax.ShapeDtypeStruct((ng, t), F32)],
        compiler_params=_cparams("parallel"),
        name="nt_proj",
    )(h, w_qt, w_vt, w_gt)


def _compress_call(xc, wa, wb, pe8, w2, *, transposed, cc):
    nc, kk = xc.shape

    def body(x_ref, wa_ref, wb_ref, pe_ref, w2_ref, o_ref):
        xv = x_ref[...]
        top = jnp.dot(xv, wa_ref[...], preferred_element_type=F32)
        bot = jnp.dot(xv, wb_ref[...], preferred_element_type=F32)
        pe_top = jnp.dot(pe_ref[...], wa_ref[...], preferred_element_type=F32)
        pe_bot = jnp.dot(pe_ref[...], wb_ref[...], preferred_element_type=F32)
        pre = top + pltpu.roll(bot, nc - 1, 0) + pe_top[0:1] + pe_bot[1:2]
        hid = (pre * _sigmoid(pre)).astype(BF16)
        if transposed:
            r = _nt_dot(w2_ref[...], hid)
            ones_rows = jnp.where(lax.broadcasted_iota(jnp.int32, (V_ROWS - HEAD_DIM, cc), 0) == 0, 1.0, 0.0)
            for c in range(nc // cc):
                o_ref[c] = jnp.concatenate([r[:, c * cc:(c + 1) * cc], ones_rows], axis=0).astype(BF16)
        else:
            o_ref[...] = jnp.dot(hid, w2_ref[...], preferred_element_type=F32).astype(BF16)

    if transposed:
        out_spec = pl.BlockSpec((None, nc // cc, V_ROWS, cc), lambda g: (g, 0, 0, 0))
        out_shape = jax.ShapeDtypeStruct((N_GROUPS, nc // cc, V_ROWS, cc), BF16)
    else:
        out_spec = pl.BlockSpec((None, nc, HEAD_DIM), lambda g: (g, 0, 0))
        out_shape = jax.ShapeDtypeStruct((N_GROUPS, nc, HEAD_DIM), BF16)
    return pl.pallas_call(
        body,
        grid=(N_GROUPS,),
        in_specs=[pl.BlockSpec((nc, kk), lambda g: (0, 0)),
                  pl.BlockSpec((None, kk, CMP_HIDDEN), lambda g: (g, 0, 0)),
                  pl.BlockSpec((None, kk, CMP_HIDDEN), lambda g: (g, 0, 0)),
                  pl.BlockSpec((8, kk), lambda g: (0, 0)),
                  pl.BlockSpec(w2.shape, lambda g: (0, 0))],
        out_specs=out_spec,
        out_shape=out_shape,
        compiler_params=_cparams("parallel"),
        name="compress_v" if transposed else "compress_k",
    )(xc, wa, wb, pe8, w2)


def _nsa_feature_tables(nc):
    def table(nrows, pos_step, coarse, onehots):
        n = np.arange(nrows, dtype=np.int32)
        pos = n * pos_step
        hi = (pos // coarse) * coarse
        lo = pos - hi
        cols = [hi, hi, hi, lo, lo, lo] + [np.zeros_like(n)] * (FEAT_ONEHOT - 6)
        if onehots:
            cols += [(n // SEL_BLOCK == b).astype(np.int32) for b in range(SLC_CHUNK // SEL_BLOCK)]
        cols += [np.zeros_like(n)] * (HEAD_DIM - len(cols))
        return jnp.asarray(np.stack(cols, axis=1).astype(np.float32)).astype(BF16)

    slc = table(SLC_CHUNK, 1, 16, True)
    win = table(WINDOW + Q_TILE, 1, 16, False)
    cmp_ = table(min(CMP_CHUNK, nc), CMP_STRIDE, 256, False)
    return slc, win, cmp_


def _nsa_call(qt, proj, kc, vct, vst, vwt, gates, slopes, qfeat, *, t, ks_col, kw_col):
    nq = t // Q_TILE
    n_sel = t // SEL_BLOCK
    nc = t // CMP_STRIDE
    cc = min(CMP_CHUNK, nc)
    top_n = min(SEL_TOP_N, n_sel)
    hg = HEADS_PER_GROUP
    wide = hg * Q_TILE
    sel_per_chunk = SLC_CHUNK // SEL_BLOCK
    win_span = WINDOW + Q_TILE
    kdim = 2 * HEAD_DIM
    tab_slc, tab_win, tab_cmp = _nsa_feature_tables(nc)

    def body(q_ref, kc_ref, vct_ref, ks_ref, vst_ref, kw_ref, vwt_ref, g_ref, sl_ref, qf_ref,
             ts_ref, tw_ref, tc_ref, o_ref,
             qt_ref, kss_ref, ksw_ref, ksc_ref, m_ref, acc_ref, p_ref, mw_ref, accw_ref, pw_ref,
             out_ref, s_ref, ps_ref, score_ref, sel_ref, flag_ref):
        g = pl.program_id(0)
        i = pl.program_id(1)
        t0 = i * Q_TILE
        tq = t0 + lax.broadcasted_iota(jnp.int32, (1, Q_TILE), 1)

        for h in range(hg):
            qt_ref[0:HEAD_DIM, h * Q_TILE:(h + 1) * Q_TILE] = q_ref[h * HEAD_DIM:(h + 1) * HEAD_DIM, :]
        qt_ref[HEAD_DIM:HEAD_DIM + FEAT_ONEHOT, :] = qf_ref[...]
        qt_ref[HEAD_DIM + FEAT_ONEHOT:kdim, :] = jnp.zeros((HEAD_DIM - FEAT_ONEHOT, wide), BF16)
        kss_ref[:, HEAD_DIM:kdim] = ts_ref[...]
        ksw_ref[:, HEAD_DIM:kdim] = tw_ref[...]
        ksc_ref[:, HEAD_DIM:kdim] = tc_ref[...]

        state_main = (m_ref, acc_ref, p_ref)
        state_win = (mw_ref, accw_ref, pw_ref)

        def reset(state):
            state[0][...] = jnp.full((1, wide), NEG_INF, F32)
            state[1][...] = jnp.zeros((V_ROWS, wide), F32)

        def chunk_offset(ref_pos):
            return sl_ref[...] * (ref_pos - t0).astype(F32)

        def row_blocks(nrows):
            return [(r, min(ROW_BLOCK, nrows - r)) for r in range(0, nrows, ROW_BLOCK)]

        def scores(kst_ref, nrows):
            return [jnp.dot(kst_ref[r:r + n, :], qt_ref[...], preferred_element_type=F32)
                    for r, n in row_blocks(nrows)]

        def softmax_pv(state, sts, vt_c, off, nrows, mask=None):
            m_ref, acc_ref, p_ref = state
            for (r, n), st in zip(row_blocks(nrows), sts):
                alphas = []
                keep = None if mask is None else mask(r, n)
                for h in range(hg):
                    sl = slice(h * Q_TILE, (h + 1) * Q_TILE)
                    s = st[:, sl]
                    if keep is not None:
                        s = jnp.where(keep, s, NEG_INF)
                    o = off[:, sl]
                    m_old = m_ref[:, sl]
                    m_new = jnp.maximum(m_old, jnp.max(s, axis=0, keepdims=True) + o)
                    alphas.append(jnp.exp2(m_old - m_new))
                    m_ref[:, sl] = m_new
                    p_ref[r:r + n, sl] = jnp.exp2(s - (m_new - o)).astype(BF16)
                pv = jnp.dot(vt_c[:, r:r + n], p_ref[r:r + n, :], preferred_element_type=F32)
                acc_ref[...] = acc_ref[...] * jnp.concatenate(alphas, axis=1) + pv

        def attend(state, kst_ref, vt_c, off, nrows, mask=None):
            softmax_pv(state, scores(kst_ref, nrows), vt_c, off, nrows, mask)

        def inv_norm(state, sl):
            valid = state[0][:, sl] > 0.5 * NEG_INF
            denom = state[1][HEAD_DIM:HEAD_DIM + 1, sl]
            return jnp.where(valid, 1.0 / jnp.maximum(denom, 1e-30), 0.0)

        def finish(state, branch, first):
            for h in range(hg):
                sl = slice(h * Q_TILE, (h + 1) * Q_TILE)
                gate = g_ref[pl.ds(branch * N_HEADS + g * hg + h, 1), :]
                contrib = state[1][0:HEAD_DIM, sl] * (inv_norm(state, sl) * gate)
                if first:
                    out_ref[:, sl] = contrib
                else:
                    out_ref[:, sl] += contrib

        reset(state_main)
        ps_ref[...] = jnp.zeros(ps_ref.shape, F32)
        n_cmp_keys = (t0 + Q_TILE - CMP_BLOCK) // CMP_STRIDE + 1
        n_cc = (n_cmp_keys + cc - 1) // cc

        def cmp_scores(c, carry):
            r0 = pl.multiple_of(c * cc, cc)
            n = r0 + lax.broadcasted_iota(jnp.int32, (cc, Q_TILE), 0)
            mask = (n * CMP_STRIDE + (CMP_BLOCK - 1)) <= tq
            off = chunk_offset(r0 * CMP_STRIDE + (CMP_BLOCK - 1))
            ksc_ref[:, 0:HEAD_DIM] = kc_ref[pl.ds(r0, cc), :]
            for hp in range(hg // 2):
                pair = slice(hp * 2 * Q_TILE, (hp + 1) * 2 * Q_TILE)
                st = jnp.dot(ksc_ref[...], qt_ref[:, pair], preferred_element_type=F32)
                for hh in range(2):
                    sl = slice((2 * hp + hh) * Q_TILE, (2 * hp + hh + 1) * Q_TILE)
                    s = jnp.where(mask, st[:, hh * Q_TILE:(hh + 1) * Q_TILE] + off[:, sl], NEG_INF)
                    s_ref[pl.ds(r0, cc), sl] = s
                    m_ref[:, sl] = jnp.maximum(m_ref[:, sl], jnp.max(s, axis=0, keepdims=True))
            return carry

        lax.fori_loop(0, n_cc, cmp_scores, 0)

        def cmp_probs(c, carry):
            r0 = pl.multiple_of(c * cc, cc)
            for hp in range(hg // 2):
                pair = slice(hp * 2 * Q_TILE, (hp + 1) * 2 * Q_TILE)
                for hh in range(2):
                    sl = slice((2 * hp + hh) * Q_TILE, (2 * hp + hh + 1) * Q_TILE)
                    p = jnp.exp2(s_ref[pl.ds(r0, cc), sl] - m_ref[:, sl])
                    s_ref[pl.ds(r0, cc), sl] = p
                    p_ref[0:cc, sl] = p.astype(BF16)
                acc_ref[:, pair] += jnp.dot(vct_ref[c], p_ref[0:cc, pair], preferred_element_type=F32)
            return carry

        lax.fori_loop(0, n_cc, cmp_probs, 0)
        finish(state_main, 0, True)

        def ps_body(c, carry):
            r0 = pl.multiple_of(c * cc, cc)
            tot = jnp.zeros((cc, Q_TILE), F32)
            for h in range(hg):
                sl = slice(h * Q_TILE, (h + 1) * Q_TILE)
                tot = tot + s_ref[pl.ds(r0, cc), sl] * inv_norm(state_main, sl)
            ps_ref[pl.ds(8 + r0, cc), :] = tot
            return carry

        lax.fori_loop(0, n_cc, ps_body, 0)

        ratio = SEL_BLOCK // CMP_STRIDE
        imp = ps_ref[pl.ds(7, n_sel, stride=ratio), :]
        for kk in range(ratio):
            imp = imp + ps_ref[pl.ds(8 + kk, n_sel, stride=ratio), :]
        jrow = lax.broadcasted_iota(jnp.int32, (n_sel, Q_TILE), 0)
        cur = tq // SEL_BLOCK
        forced = jnp.where(jrow == 0, 1.0, jnp.where(jrow == cur, 1.0, jnp.where(jrow == cur - 1, 1.0, 0.0)))
        score_ref[...] = jnp.where(forced > 0.5, -2.0, jnp.where(jrow <= cur, imp, -1.0))
        sel_ref[...] = forced
        jrow_f = jrow.astype(F32)

        def run_topk(nrows):
            jr = jrow_f[0:nrows, :]

            def topk_body(r, carry):
                sc = score_ref[0:nrows, :]
                mx = jnp.max(sc, axis=0, keepdims=True)
                first = jnp.min(jnp.where(sc == mx, jr, float(n_sel)), axis=0, keepdims=True)
                pick = jr == first
                sel_ref[0:nrows, :] = jnp.where(pick, 1.0, sel_ref[0:nrows, :])
                score_ref[0:nrows, :] = jnp.where(pick, -2.0, sc)
                return carry

            lax.fori_loop(0, top_n - N_FORCED, topk_body, 0)

        half_rows = n_sel // 2
        tile_in_first_half = (t0 + Q_TILE - 1) // SEL_BLOCK < half_rows

        @pl.when(tile_in_first_half)
        def _():
            run_topk(half_rows)

        @pl.when(jnp.logical_not(tile_in_first_half))
        def _():
            run_topk(n_sel)
        for c in range(t // SLC_CHUNK):
            blocks = sel_ref[c * sel_per_chunk:(c + 1) * sel_per_chunk, :]
            flag_ref[c] = (jnp.max(blocks) > 0.5).astype(jnp.int32)

        reset(state_main)
        last_sc = (t0 + Q_TILE - 1) // SLC_CHUNK

        def slc_stage(c):
            r0 = pl.multiple_of(c * SLC_CHUNK, SLC_CHUNK)
            kss_ref[:, 0:HEAD_DIM] = ks_ref[pl.ds(r0, SLC_CHUNK), :]
            sel8 = sel_ref[pl.ds(pl.multiple_of(c * sel_per_chunk, sel_per_chunk), sel_per_chunk), :]
            bias = (sel8 - 1.0) * MASK_BIG
            rows = jnp.concatenate([jnp.concatenate([bias] * hg, axis=1),
                                    jnp.zeros((FEAT_ONEHOT - sel_per_chunk, wide), F32)], axis=0)
            qt_ref[HEAD_DIM + FEAT_ONEHOT:HEAD_DIM + 2 * FEAT_ONEHOT, :] = rows.astype(BF16)
            return r0

        def slc_body(c, carry):
            @pl.when(flag_ref[c] > 0)
            def _():
                r0 = slc_stage(c)
                attend(state_main, kss_ref, vst_ref[c], chunk_offset(r0), SLC_CHUNK)

            return carry

        lax.fori_loop(0, last_sc, slc_body, 0)

        reset(state_win)
        r0s = slc_stage(last_sc)
        n_wc = win_span // WIN_CHUNK
        c0 = jnp.maximum(i * (Q_TILE // WIN_CHUNK) - WINDOW // WIN_CHUNK, 0)
        r0w = pl.multiple_of(c0 * WIN_CHUNK, WIN_CHUNK)
        ksw_ref[:, 0:HEAD_DIM] = kw_ref[pl.ds(r0w, win_span), :]
        sts_slc = scores(kss_ref, SLC_CHUNK)
        sts_win = scores(ksw_ref, win_span)

        def masks(r, n):
            return (r0s + r + lax.broadcasted_iota(jnp.int32, (n, Q_TILE), 0)) <= tq

        softmax_pv(state_main, sts_slc, vst_ref[last_sc], chunk_offset(r0s), SLC_CHUNK, masks)
        finish(state_main, 1, False)

        def maskw(r, n):
            dpos = (r0w + r + lax.broadcasted_iota(jnp.int32, (n, Q_TILE), 0)) - tq
            return jnp.where(dpos <= 0, dpos, -2 * WINDOW) > -WINDOW

        vtw = jnp.concatenate([vwt_ref[c0 + kk] for kk in range(n_wc)], axis=1)
        softmax_pv(state_win, sts_win, vtw, chunk_offset(r0w), win_span, maskw)
        finish(state_win, 2, False)

        for h in range(hg):
            sl = slice(h * Q_TILE, (h + 1) * Q_TILE)
            o_ref[:, h * HEAD_DIM:(h + 1) * HEAD_DIM] = out_ref[:, sl].T.astype(o_ref.dtype)

    one = pl.Buffered(1)
    in_specs = [
        pl.BlockSpec((hg * HEAD_DIM, Q_TILE), lambda g, i: (g, i)),
        pl.BlockSpec((None, nc, HEAD_DIM), lambda g, i: (g, 0, 0)),
        pl.BlockSpec((None, nc // cc, V_ROWS, cc), lambda g, i: (g, 0, 0, 0)),
        pl.BlockSpec((t, HEAD_DIM), lambda g, i: (0, ks_col + g), pipeline_mode=one),
        pl.BlockSpec((t // SLC_CHUNK, V_ROWS, SLC_CHUNK), lambda g, i: (0, g, 0), pipeline_mode=one),
        pl.BlockSpec((t, HEAD_DIM), lambda g, i: (0, kw_col + g), pipeline_mode=one),
        pl.BlockSpec((t // WIN_CHUNK, V_ROWS, WIN_CHUNK), lambda g, i: (0, g, 0), pipeline_mode=one),
        pl.BlockSpec((gates.shape[0], Q_TILE), lambda g, i: (0, i)),
        pl.BlockSpec((None, 1, wide), lambda g, i: (g, 0, 0)),
        pl.BlockSpec((None, FEAT_ONEHOT, wide), lambda g, i: (g, 0, 0)),
        pl.BlockSpec(tab_slc.shape, lambda g, i: (0, 0)),
        pl.BlockSpec(tab_win.shape, lambda g, i: (0, 0)),
        pl.BlockSpec(tab_cmp.shape, lambda g, i: (0, 0)),
    ]
    scratch = [
        pltpu.VMEM((kdim, wide), BF16),
        pltpu.VMEM((SLC_CHUNK, kdim), BF16),
        pltpu.VMEM((win_span, kdim), BF16),
        pltpu.VMEM((cc, kdim), BF16),
        pltpu.VMEM((1, wide), F32),
        pltpu.VMEM((V_ROWS, wide), F32),
        pltpu.VMEM((SLC_CHUNK, wide), BF16),
        pltpu.VMEM((1, wide), F32),
        pltpu.VMEM((V_ROWS, wide), F32),
        pltpu.VMEM((win_span, wide), BF16),
        pltpu.VMEM((HEAD_DIM, wide), F32),
        pltpu.VMEM((nc, wide), F32),
        pltpu.VMEM((8 + nc, Q_TILE), F32),
        pltpu.VMEM((n_sel, Q_TILE), F32),
        pltpu.VMEM((n_sel, Q_TILE), F32),
        pltpu.SMEM((t // SLC_CHUNK,), jnp.int32),
    ]
    return pl.pallas_call(
        body,
        grid=(N_GROUPS, nq),
        in_specs=in_specs,
        out_specs=pl.BlockSpec((Q_TILE, hg * HEAD_DIM), lambda g, i: (i, g)),
        out_shape=jax.ShapeDtypeStruct((t, N_HEADS * HEAD_DIM), BF16),
        scratch_shapes=scratch,
        compiler_params=_cparams("parallel", "parallel"),
        name="nsa",
    )(qt, kc, vct, proj, vst, proj, vwt, gates, slopes, qfeat, tab_slc, tab_win, tab_cmp)


def _glu_proj_call(h, w_glu):
    t, d = h.shape
    ch = w_glu.shape[1] // 2
    tm = min(1024, t)
    tn = 512
    nj = ch // tn

    def body(h_ref, wa_ref, wg_ref, o_ref):
        hv = h_ref[...]
        a = jnp.dot(hv, wa_ref[...], preferred_element_type=F32)
        g = jnp.dot(hv, wg_ref[...], preferred_element_type=F32)
        o_ref[...] = (a * _sigmoid(g)).astype(o_ref.dtype)

    return pl.pallas_call(
        body,
        grid=(t // tm, nj),
        in_specs=[pl.BlockSpec((tm, d), lambda i, j: (i, 0)),
                  pl.BlockSpec((d, tn), lambda i, j: (0, j)),
                  pl.BlockSpec((d, tn), lambda i, j: (0, nj + j))],
        out_specs=pl.BlockSpec((tm, tn), lambda i, j: (i, j)),
        out_shape=jax.ShapeDtypeStruct((t, ch), BF16),
        compiler_params=_cparams("parallel", "parallel"),
        name="proj_glu",
    )(h, w_glu, w_glu)


def _conv_front_call(u, w_dw, b_dw, ln_g, ln_b):
    t = u.shape[0]
    ch = w_dw.shape[1]
    tm = min(256, t)
    halo = 32
    rb = 32
    strip = 512
    w_rep = jnp.repeat(w_dw, F32_SUBLANES, axis=0)

    def body(a_ref, ah_ref, w_ref, b_ref, lg_ref, lb_ref, o_ref, u_ref, sb_ref, y_ref):
        i = pl.program_id(0)
        u_ref[halo:halo + tm, :] = a_ref[...].astype(F32)
        u_ref[0:halo, :] = jnp.where(i > 0, ah_ref[...].astype(F32), 0.0)
        for cs in range(ch // strip):
            cols = slice(cs * strip, (cs + 1) * strip)
            ue = u_ref[:, cols]
            sb_ref[0] = ue
            for part in range(1, F32_SUBLANES):
                sb_ref[part] = pltpu.roll(ue, part, 0)

            def rows(r, carry, cols=cols):
                r0 = pl.multiple_of(r * rb, rb)
                acc = jnp.zeros((rb // F32_SUBLANES, F32_SUBLANES, strip), F32)
                for s in range(CONV_WIDTH):
                    whole, part = divmod(s, F32_SUBLANES)
                    k = CONV_WIDTH - 1 - s
                    tap = sb_ref[part, pl.ds(halo + r0 - F32_SUBLANES * whole, rb), :]
                    wk = w_ref[k * F32_SUBLANES:(k + 1) * F32_SUBLANES, cols]
                    acc = acc + tap.reshape(rb // F32_SUBLANES, F32_SUBLANES, strip) * wk
                y_ref[pl.ds(r0, rb), cols] = acc.reshape(rb, strip) + b_ref[:, cols]
                return carry

            lax.fori_loop(0, tm // rb, rows, 0)
        y = y_ref[...]
        mu = jnp.mean(y, axis=-1, keepdims=True)
        yc = y - mu
        var = jnp.mean(yc * yc, axis=-1, keepdims=True)
        z = yc * lax.rsqrt(var + EPS) * lg_ref[...] + lb_ref[...]
        o_ref[...] = (z * _sigmoid(z)).astype(o_ref.dtype)

    per = tm // halo
    return pl.pallas_call(
        body,
        grid=(t // tm,),
        in_specs=[pl.BlockSpec((tm, ch), lambda i: (i, 0)),
                  pl.BlockSpec((halo, ch), lambda i: (jnp.maximum(i * per - 1, 0), 0)),
                  pl.BlockSpec((CONV_WIDTH * F32_SUBLANES, ch), lambda i: (0, 0)),
                  pl.BlockSpec((1, ch), lambda i: (0, 0)),
                  pl.BlockSpec((1, ch), lambda i: (0, 0)),
                  pl.BlockSpec((1, ch), lambda i: (0, 0))],
        out_specs=pl.BlockSpec((tm, ch), lambda i: (i, 0)),
        out_shape=jax.ShapeDtypeStruct((t, ch), BF16),
        scratch_shapes=[pltpu.VMEM((halo + tm, ch), F32),
                        pltpu.VMEM((F32_SUBLANES, halo + tm, strip), F32),
                        pltpu.VMEM((tm, ch), F32)],
        compiler_params=_cparams("parallel"),
        name="conv_front",
    )(u, u, w_rep, b_dw.reshape(1, ch), ln_g.reshape(1, ch), ln_b.reshape(1, ch))


def _ffn_up_call(h2, w_up, w_dw, b_dw):
    t, d = h2.shape
    n2 = w_up.shape[1]
    f = n2 // 2
    tm = min(1024, t)
    rsub = 256
    tn = 512
    nj = f // tn
    sub = F32_SUBLANES
    w_pad = jnp.concatenate([w_dw, jnp.zeros((sub - FFN_CONV_WIDTH, n2), F32)], axis=0)
    b2 = b_dw.reshape(1, n2)

    def body(h_ref, wa_ref, wg_ref, da_ref, dg_ref, ba_ref, bg_ref, o_ref, ca_ref, cg_ref):
        i = pl.program_id(0)
        j = pl.program_id(1)

        @pl.when(i == 0)
        def _():
            ca_ref[j] = jnp.zeros((sub, tn), F32)
            cg_ref[j] = jnp.zeros((sub, tn), F32)

        wa = wa_ref[...]
        wg = wg_ref[...]

        def conv(up, prev, d_ref, b_ref):
            ext = jnp.concatenate([prev, up], axis=0)
            acc = b_ref[...] + up * d_ref[FFN_CONV_WIDTH - 1:FFN_CONV_WIDTH, :]
            for s in range(1, FFN_CONV_WIDTH):
                k = FFN_CONV_WIDTH - 1 - s
                acc = acc + pltpu.roll(ext, s, 0)[sub:, :] * d_ref[k:k + 1, :]
            return acc

        def project(rs):
            hv = h_ref[rs * rsub:(rs + 1) * rsub, :]
            return (jnp.dot(hv, wa, preferred_element_type=F32),
                    jnp.dot(hv, wg, preferred_element_type=F32))

        prev_a = ca_ref[j]
        prev_g = cg_ref[j]
        n_rs = tm // rsub
        ups = project(0)
        for rs in range(n_rs):
            rows = slice(rs * rsub, (rs + 1) * rsub)
            up_a, up_g = ups
            if rs + 1 < n_rs:
                ups = project(rs + 1)
            ca = conv(up_a, prev_a, da_ref, ba_ref)
            cg = conv(up_g, prev_g, dg_ref, bg_ref)
            o_ref[rows, :] = (cg * _sigmoid(cg) * ca).astype(o_ref.dtype)
            prev_a = up_a[rsub - sub:, :]
            prev_g = up_g[rsub - sub:, :]
        ca_ref[j] = prev_a
        cg_ref[j] = prev_g

    return pl.pallas_call(
        body,
        grid=(t // tm, nj),
        in_specs=[pl.BlockSpec((tm, d), lambda i, j: (i, 0)),
                  pl.BlockSpec((d, tn), lambda i, j: (0, j)),
                  pl.BlockSpec((d, tn), lambda i, j: (0, nj + j)),
                  pl.BlockSpec((sub, tn), lambda i, j: (0, j)),
                  pl.BlockSpec((sub, tn), lambda i, j: (0, nj + j)),
                  pl.BlockSpec((1, tn), lambda i, j: (0, j)),
                  pl.BlockSpec((1, tn), lambda i, j: (0, nj + j))],
        out_specs=pl.BlockSpec((tm, tn), lambda i, j: (i, j)),
        out_shape=jax.ShapeDtypeStruct((t, f), BF16),
        scratch_shapes=[pltpu.VMEM((nj, sub, tn), F32), pltpu.VMEM((nj, sub, tn), F32)],
        compiler_params=_cparams("arbitrary", "arbitrary"),
        name="ffn_up",
    )(h2, w_up, w_up, w_pad, w_pad, b2, b2)


def _layer(x, c_row, p):
    t, d = x.shape
    ada = _ada_call(c_row, p["w_ada"], p["b_ada"])

    w_in = p["w_in"]
    qw = N_HEADS * HEAD_DIM
    kvw = N_GROUPS * HEAD_DIM
    o = [0, qw]
    for _ in range(6):
        o.append(o[-1] + kvw)
    o.append(o[-1] + 3 * N_HEADS)
    o.append(o[-1] + 2 * d)
    o.append(o[-1] + 2 * d)
    w_q, w_kc, w_vc, w_ks, w_vs, w_kw, w_vw, w_gn, w_glu, w_mg = [w_in[:, o[k]:o[k + 1]] for k in range(10)]
    w_rm = jnp.concatenate([w_kc, w_vc, w_ks, w_kw, w_mg], axis=1).astype(BF16)
    w_glu = w_glu.astype(BF16)
    w_t = _transpose_cast_call(w_in, [(o[0], qw), (o[4], kvw), (o[6], kvw), (o[7], 256)])
    w_qt = w_t[0:qw]
    w_vt = w_t[qw:qw + 2 * kvw]
    n_gate_rows = 64
    w_gt = w_t[qw + 2 * kvw:qw + 2 * kvw + n_gate_rows]

    h1 = _norm_call(x, p["norm1_g"], ada, shift_idx=0, scale_idx=1)
    proj = _matmul_call(h1, w_rm, tm=1024, tn=1024, out_dtype=BF16, name="proj_rm")
    u = _glu_proj_call(h1, w_glu)
    qt, vst, vwt, gates = _nt_proj_call(h1, w_qt, w_vt, w_gt, HEAD_DIM ** -0.5 * LOG2E)

    nc = t // CMP_STRIDE
    cc = min(CMP_CHUNK, nc)
    half = CMP_BLOCK // CMP_STRIDE
    assert half == 2

    def expand(w1):
        w1r = w1.reshape(half, CMP_STRIDE, HEAD_DIM, CMP_HIDDEN)
        outs = []
        for hf in range(half):
            per_g = []
            for g in range(N_GROUPS):
                z = jnp.zeros((CMP_STRIDE, N_GROUPS, HEAD_DIM, CMP_HIDDEN), F32).at[:, g].set(w1r[hf])
                per_g.append(z.reshape(CMP_STRIDE * kvw, CMP_HIDDEN))
            outs.append(jnp.stack(per_g).astype(BF16))
        return outs

    pe = p["cmp_pe"].reshape(half, CMP_STRIDE, 1, HEAD_DIM)
    pe_rows = jnp.broadcast_to(pe, (half, CMP_STRIDE, N_GROUPS, HEAD_DIM)).reshape(half, CMP_STRIDE * kvw)
    pe8 = jnp.concatenate([pe_rows, jnp.zeros((8 - half, CMP_STRIDE * kvw), F32)], axis=0).astype(BF16)
    kc_raw = proj[:, 0:kvw].reshape(nc, CMP_STRIDE * kvw)
    vc_raw = proj[:, kvw:2 * kvw].reshape(nc, CMP_STRIDE * kvw)
    wka, wkb = expand(p["w_kc1"])
    wva, wvb = expand(p["w_vc1"])
    kc = _compress_call(kc_raw, wka, wkb, pe8, p["w_kc2"].astype(BF16), transposed=False, cc=cc)
    vct = _compress_call(vc_raw, wva, wvb, pe8, p["w_vc2"].T.astype(BF16), transposed=True, cc=cc)

    head_ids = np.arange(1, N_HEADS + 1, dtype=np.float64)
    slopes_np = (np.exp2(-8.0 * head_ids / N_HEADS) * LOG2E).astype(np.float32)
    slopes_np = np.broadcast_to(slopes_np.reshape(N_GROUPS, 1, HEADS_PER_GROUP, 1),
                                (N_GROUPS, 1, HEADS_PER_GROUP, Q_TILE)).reshape(N_GROUPS, 1, HEADS_PER_GROUP * Q_TILE)
    s1 = slopes_np.astype(BF16)
    s2 = (slopes_np - s1.astype(np.float32)).astype(BF16)
    s3 = (slopes_np - s1.astype(np.float32) - s2.astype(np.float32)).astype(BF16)
    zero_rows = np.zeros((N_GROUPS, FEAT_ONEHOT - 6, slopes_np.shape[-1]), dtype=s1.dtype)
    qfeat = jnp.asarray(np.concatenate([s1, s2, s3, s1, s2, s3, zero_rows], axis=1))
    slopes = jnp.asarray(slopes_np)
    o_nsa = _nsa_call(qt, proj, kc, vct, vst, vwt, gates, slopes, qfeat, t=t,
                      ks_col=(2 * kvw) // HEAD_DIM, kw_col=(3 * kvw) // HEAD_DIM)

    u2 = _conv_front_call(u, p["conv_dw_w"], p["conv_dw_b"], p["conv_ln_g"], p["conv_ln_b"])

    merged = _merge_call(o_nsa, p["w_o_nsa"].astype(BF16), u2, p["conv_pw_w"].astype(BF16), p["conv_pw_b"],
                         proj, gate_col=4 * kvw)
    x1, h2 = _out_norm_call(merged, p["w_out"].astype(BF16), x, ada, p["norm2_g"],
                            gate_idx=2, shift_idx=3, scale_idx=4)
    act = _ffn_up_call(h2, p["ffn_w_up"].astype(BF16), p["ffn_dw_w"], p["ffn_dw_b"])
    x2 = _matmul_call(
        act, p["ffn_w_down"].astype(BF16), tm=512, tn=512, out_dtype=F32, name="ffn_down",
        epilogue=lambda acc, x_ref, g_ref: x_ref[...] + g_ref[0:1, :] * acc,
        extras=(x1, ada),
        extra_specs=(pl.BlockSpec((512, 512), lambda i, j: (i, j)),
                     pl.BlockSpec((8, 512), lambda i, j: (0, 5 * (d // 512) + j))))
    return x2


def kernel(x, c, w_ada, b_ada, norm1_g, w_in, cmp_pe, w_kc1, w_kc2, w_vc1, w_vc2, w_o_nsa, conv_dw_w, conv_dw_b, conv_ln_g, conv_ln_b, conv_pw_w, conv_pw_b, w_out, norm2_g, ffn_w_up, ffn_dw_w, ffn_dw_b, ffn_w_down, final_g):
    b, t, d = x.shape
    stacked = dict(w_ada=w_ada, b_ada=b_ada, norm1_g=norm1_g, w_in=w_in, cmp_pe=cmp_pe, w_kc1=w_kc1,
                   w_kc2=w_kc2, w_vc1=w_vc1, w_vc2=w_vc2, w_o_nsa=w_o_nsa, conv_dw_w=conv_dw_w,
                   conv_dw_b=conv_dw_b, conv_ln_g=conv_ln_g, conv_ln_b=conv_ln_b, conv_pw_w=conv_pw_w,
                   conv_pw_b=conv_pw_b, w_out=w_out, norm2_g=norm2_g, ffn_w_up=ffn_w_up,
                   ffn_dw_w=ffn_dw_w, ffn_dw_b=ffn_dw_b, ffn_w_down=ffn_w_down)
    depth = w_ada.shape[0]
    xs = x.reshape(b * t, d)
    outs = []
    for bi in range(b):
        xb = xs if b == 1 else xs[bi * t:(bi + 1) * t]
        for layer in range(depth):
            xb = _layer(xb, c[bi:bi + 1], {k: v[layer] for k, v in stacked.items()})
        outs.append(_norm_call(xb, final_g, out_dtype=x.dtype))
    out = outs[0] if b == 1 else jnp.concatenate(outs, axis=0)
    return out.reshape(b, t, d)
```

```python
import functools

import jax
import jax.numpy as jnp
import numpy as np
from jax import lax
from jax.experimental import pallas as pl
from jax.experimental.pallas import tpu as pltpu

F32 = jnp.float32
BF16 = jnp.bfloat16

N_HEADS = 16
N_GROUPS = 2
HEADS_PER_GROUP = N_HEADS // N_GROUPS
HEAD_DIM = 128
CMP_BLOCK = 32
CMP_STRIDE = 16
CMP_HIDDEN = 2 * HEAD_DIM
SEL_BLOCK = 64
SEL_TOP_N = 16
WINDOW = 512
FORCE_BONUS = 1e6
CONV_WIDTH = 31
FFN_CONV_WIDTH = 3
EPS = 1e-6
NEG_INF = -1e30
LOG2E = 1.4426950408889634

V7X_VMEM_BYTES = 64 * 1024 * 1024
VMEM_LIMIT_BYTES = V7X_VMEM_BYTES - 8 * 1024 * 1024
LANES = 128
F32_SUBLANES = 8

Q_TILE = 128
SLC_CHUNK = 512
WIN_CHUNK = 128
CMP_CHUNK = 256
FEAT_ONEHOT = 16
MASK_BIG = 2.0 ** 20
N_FORCED = 3
ROW_BLOCK = 256
V_ROWS = HEAD_DIM + 16


def _cparams(*sem):
    return pltpu.CompilerParams(dimension_semantics=sem, vmem_limit_bytes=VMEM_LIMIT_BYTES)


def _sigmoid(v):
    return 1.0 / (1.0 + jnp.exp(-v))


def _nt_dot(a, b):
    return lax.dot_general(a, b, (((1,), (1,)), ((), ())), preferred_element_type=F32)


def _ada_call(c_row, w, b):
    d, n = w.shape
    tn = 1024
    cb = jnp.broadcast_to(c_row.reshape(d, 1), (d, LANES))

    def body(cb_ref, w_ref, b_ref, o_ref):
        cv = cb_ref[...]
        act = cv * _sigmoid(cv)
        for j in range(tn // LANES):
            sl = slice(j * LANES, (j + 1) * LANES)
            row = jnp.sum(w_ref[:, sl] * act, axis=0, keepdims=True) + b_ref[:, sl]
            o_ref[:, sl] = jnp.broadcast_to(row, (8, LANES))

    return pl.pallas_call(
        body,
        grid=(n // tn,),
        in_specs=[pl.BlockSpec((d, LANES), lambda j: (0, 0)),
                  pl.BlockSpec((d, tn), lambda j: (0, j)),
                  pl.BlockSpec((1, tn), lambda j: (0, j))],
        out_specs=pl.BlockSpec((8, tn), lambda j: (0, j)),
        out_shape=jax.ShapeDtypeStruct((8, n), F32),
        compiler_params=_cparams("parallel"),
        name="ada",
    )(cb, w, b.reshape(1, n))


def _norm_call(x, gain, ada=None, shift_idx=0, scale_idx=0, out_dtype=BF16):
    t, d = x.shape
    tm = 512

    def body(*refs):
        if ada is None:
            x_ref, g_ref, o_ref = refs
        else:
            x_ref, g_ref, sh_ref, sc_ref, o_ref = refs
        xv = x_ref[...]
        y = xv * lax.rsqrt(jnp.mean(xv * xv, axis=-1, keepdims=True) + EPS) * g_ref[...]
        if ada is not None:
            y = y * (1.0 + sc_ref[0:1, :]) + sh_ref[0:1, :]
        o_ref[...] = y.astype(o_ref.dtype)

    in_specs = [pl.BlockSpec((tm, d), lambda i: (i, 0)), pl.BlockSpec((1, d), lambda i: (0, 0))]
    args = [x, gain.reshape(1, d)]
    if ada is not None:
        in_specs += [pl.BlockSpec((8, d), lambda i: (0, shift_idx)),
                     pl.BlockSpec((8, d), lambda i: (0, scale_idx))]
        args += [ada, ada]
    return pl.pallas_call(
        body,
        grid=(t // tm,),
        in_specs=in_specs,
        out_specs=pl.BlockSpec((tm, d), lambda i: (i, 0)),
        out_shape=jax.ShapeDtypeStruct((t, d), out_dtype),
        compiler_params=_cparams("parallel"),
        name="norm",
    )(*args)


def _matmul_call(a, w, *, tm, tn, out_dtype, name, epilogue=None, extras=(), extra_specs=()):
    m, k = a.shape
    n = w.shape[1]
    tm = min(tm, m)

    def body(a_ref, w_ref, *rest):
        o_ref = rest[-1]
        acc = jnp.dot(a_ref[...], w_ref[...].astype(BF16), preferred_element_type=F32)
        if epilogue is not None:
            acc = epilogue(acc, *rest[:-1])
        o_ref[...] = acc.astype(o_ref.dtype)

    return pl.pallas_call(
        body,
        grid=(m // tm, n // tn),
        in_specs=[pl.BlockSpec((tm, k), lambda i, j: (i, 0)),
                  pl.BlockSpec((k, tn), lambda i, j: (0, j))] + list(extra_specs),
        out_specs=pl.BlockSpec((tm, tn), lambda i, j: (i, j)),
        out_shape=jax.ShapeDtypeStruct((m, n), out_dtype),
        compiler_params=_cparams("parallel", "parallel"),
        name=name,
    )(a, w, *extras)


def _transpose_cast_call(w, col_ranges):
    k = w.shape[0]
    tc = 256
    blocks = []
    for start, width in col_ranges:
        assert start % tc == 0 and width % tc == 0, (start, width)
        blocks += [start // tc + b for b in range(width // tc)]
    n_blocks = len(blocks)

    def col_block(i):
        idx = jnp.int32(blocks[0])
        for pos in range(1, n_blocks):
            idx = jnp.where(i >= pos, jnp.int32(blocks[pos]), idx)
        return idx

    def body(w_ref, o_ref):
        o_ref[...] = w_ref[...].T.astype(o_ref.dtype)

    return pl.pallas_call(
        body,
        grid=(n_blocks,),
        in_specs=[pl.BlockSpec((k, tc), lambda i: (0, col_block(i)))],
        out_specs=pl.BlockSpec((tc, k), lambda i: (i, 0)),
        out_shape=jax.ShapeDtypeStruct((n_blocks * tc, k), BF16),
        compiler_params=_cparams("parallel"),
        name="transpose_cast",
    )(w)


def _merge_call(a, wa, b, wb, bias, gates, *, gate_col):
    t, ka = a.shape
    kb = b.shape[1]
    d = wa.shape[1]
    tm = min(512, t)
    tn = 1024
    nb = d // tn
    g0 = gate_col // tn

    def body(a_ref, wa_ref, b_ref, wb_ref, bias_ref, ga_ref, gb_ref, o_ref):
        ya = jnp.dot(a_ref[...], wa_ref[...], preferred_element_type=F32)
        yb = jnp.dot(b_ref[...], wb_ref[...], preferred_element_type=F32) + bias_ref[...]
        o_ref[...] = (_sigmoid(ga_ref[...].astype(F32)) * ya
                      + _sigmoid(gb_ref[...].astype(F32)) * yb).astype(o_ref.dtype)

    return pl.pallas_call(
        body,
        grid=(t // tm, nb),
        in_specs=[pl.BlockSpec((tm, ka), lambda i, j: (i, 0)),
                  pl.BlockSpec((ka, tn), lambda i, j: (0, j)),
                  pl.BlockSpec((tm, kb), lambda i, j: (i, 0)),
                  pl.BlockSpec((kb, tn), lambda i, j: (0, j)),
                  pl.BlockSpec((1, tn), lambda i, j: (0, j)),
                  pl.BlockSpec((tm, tn), lambda i, j: (i, g0 + j)),
                  pl.BlockSpec((tm, tn), lambda i, j: (i, g0 + nb + j))],
        out_specs=pl.BlockSpec((tm, tn), lambda i, j: (i, j)),
        out_shape=jax.ShapeDtypeStruct((t, d), BF16),
        compiler_params=_cparams("parallel", "parallel"),
        name="merge",
    )(a, wa, b, wb, bias.reshape(1, d), gates, gates)


def _out_norm_call(a, w, x, ada, gain, *, gate_idx, shift_idx, scale_idx):
    t, d = x.shape
    tm = min(512, t)

    def body(a_ref, w_ref, x_ref, g_ref, sh_ref, sc_ref, n_ref, x1_ref, h_ref):
        acc = jnp.dot(a_ref[...], w_ref[...], preferred_element_type=F32)
        x1 = x_ref[...] + g_ref[0:1, :] * acc
        x1_ref[...] = x1
        y = x1 * lax.rsqrt(jnp.mean(x1 * x1, axis=-1, keepdims=True) + EPS) * n_ref[...]
        h_ref[...] = (y * (1.0 + sc_ref[0:1, :]) + sh_ref[0:1, :]).astype(h_ref.dtype)

    return pl.pallas_call(
        body,
        grid=(t // tm,),
        in_specs=[pl.BlockSpec((tm, a.shape[1]), lambda i: (i, 0)),
                  pl.BlockSpec(w.shape, lambda i: (0, 0), pipeline_mode=pl.Buffered(1)),
                  pl.BlockSpec((tm, d), lambda i: (i, 0)),
                  pl.BlockSpec((8, d), lambda i: (0, gate_idx)),
                  pl.BlockSpec((8, d), lambda i: (0, shift_idx)),
                  pl.BlockSpec((8, d), lambda i: (0, scale_idx)),
                  pl.BlockSpec((1, d), lambda i: (0, 0))],
        out_specs=[pl.BlockSpec((tm, d), lambda i: (i, 0)),
                   pl.BlockSpec((tm, d), lambda i: (i, 0))],
        out_shape=[jax.ShapeDtypeStruct((t, d), F32), jax.ShapeDtypeStruct((t, d), BF16)],
        compiler_params=_cparams("parallel"),
        name="out_proj_norm",
    )(a, w, x, ada, ada, ada, gain.reshape(1, d))


def _nt_proj_call(h, w_qt, w_vt, w_gt, q_scale):
    t, d = h.shape
    tq = SLC_CHUNK
    nqr = w_qt.shape[0]
    nv = w_vt.shape[0] // 2
    ng = w_gt.shape[0]
    per = tq // WIN_CHUNK

    def body(h_ref, wq_ref, wv_ref, wg_ref, q_ref, vs_ref, vw_ref, g_ref):
        hv = h_ref[...]
        q_ref[...] = (_nt_dot(wq_ref[...], hv) * q_scale).astype(BF16)
        r = _nt_dot(wv_ref[...], hv)
        ones_rows = jnp.where(lax.broadcasted_iota(jnp.int32, (V_ROWS - HEAD_DIM, tq), 0) == 0, 1.0, 0.0)
        for g in range(N_GROUPS):
            full_s = jnp.concatenate([r[g * HEAD_DIM:(g + 1) * HEAD_DIM], ones_rows], axis=0).astype(BF16)
            vs_ref[0, g * V_ROWS:(g + 1) * V_ROWS, :] = full_s
            full_w = jnp.concatenate([r[nv + g * HEAD_DIM:nv + (g + 1) * HEAD_DIM], ones_rows], axis=0).astype(BF16)
            for kk in range(per):
                vw_ref[kk, g * V_ROWS:(g + 1) * V_ROWS, :] = full_w[:, kk * WIN_CHUNK:(kk + 1) * WIN_CHUNK]
        g_ref[...] = _sigmoid(_nt_dot(wg_ref[...], hv))

    return pl.pallas_call(
        body,
        grid=(t // tq,),
        in_specs=[pl.BlockSpec((tq, d), lambda i: (i, 0)),
                  pl.BlockSpec((nqr, d), lambda i: (0, 0)),
                  pl.BlockSpec((2 * nv, d), lambda i: (0, 0)),
                  pl.BlockSpec((ng, d), lambda i: (0, 0))],
        out_specs=[pl.BlockSpec((nqr, tq), lambda i: (0, i)),
                   pl.BlockSpec((1, N_GROUPS * V_ROWS, tq), lambda i: (i, 0, 0)),
                   pl.BlockSpec((per, N_GROUPS * V_ROWS, WIN_CHUNK), lambda i: (i, 0, 0)),
                   pl.BlockSpec((ng, tq), lambda i: (0, i))],
        out_shape=[jax.ShapeDtypeStruct((nqr, t), BF16),
                   jax.ShapeDtypeStruct((t // tq, N_GROUPS * V_ROWS, tq), BF16),
                   jax.ShapeDtypeStruct((t // WIN_CHUNK, N_GROUPS * V_ROWS, WIN_CHUNK), BF16),
                   jax.ShapeDtypeStruct((ng, t), F32)],
        compiler_params=_cparams("parallel"),
        name="nt_proj",
    )(h, w_qt, w_vt, w_gt)


def _compress_call(xc, wa, wb, pe8, w2, *, transposed, cc):
    nc, kk = xc.shape

    def body(x_ref, wa_ref, wb_ref, pe_ref, w2_ref, o_ref):
        xv = x_ref[...]
        top = jnp.dot(xv, wa_ref[...], preferred_element_type=F32)
        bot = jnp.dot(xv, wb_ref[...], preferred_element_type=F32)
        pe_top = jnp.dot(pe_ref[...], wa_ref[...], preferred_element_type=F32)
        pe_bot = jnp.dot(pe_ref[...], wb_ref[...], preferred_element_type=F32)
        pre = top + pltpu.roll(bot, nc - 1, 0) + pe_top[0:1] + pe_bot[1:2]
        hid = (pre * _sigmoid(pre)).astype(BF16)
        if transposed:
            r = _nt_dot(w2_ref[...], hid)
            ones_rows = jnp.where(lax.broadcasted_iota(jnp.int32, (V_ROWS - HEAD_DIM, cc), 0) == 0, 1.0, 0.0)
            for c in range(nc // cc):
                o_ref[c] = jnp.concatenate([r[:, c * cc:(c + 1) * cc], ones_rows], axis=0).astype(BF16)
        else:
            o_ref[...] = jnp.dot(hid, w2_ref[...], preferred_element_type=F32).astype(BF16)

    if transposed:
        out_spec = pl.BlockSpec((None, nc // cc, V_ROWS, cc), lambda g: (g, 0, 0, 0))
        out_shape = jax.ShapeDtypeStruct((N_GROUPS, nc // cc, V_ROWS, cc), BF16)
    else:
        out_spec = pl.BlockSpec((None, nc, HEAD_DIM), lambda g: (g, 0, 0))
        out_shape = jax.ShapeDtypeStruct((N_GROUPS, nc, HEAD_DIM), BF16)
    return pl.pallas_call(
        body,
        grid=(N_GROUPS,),
        in_specs=[pl.BlockSpec((nc, kk), lambda g: (0, 0)),
                  pl.BlockSpec((None, kk, CMP_HIDDEN), lambda g: (g, 0, 0)),
                  pl.BlockSpec((None, kk, CMP_HIDDEN), lambda g: (g, 0, 0)),
                  pl.BlockSpec((8, kk), lambda g: (0, 0)),
                  pl.BlockSpec(w2.shape, lambda g: (0, 0))],
        out_specs=out_spec,
        out_shape=out_shape,
        compiler_params=_cparams("parallel"),
        name="compress_v" if transposed else "compress_k",
    )(xc, wa, wb, pe8, w2)


def _nsa_feature_tables(nc):
    def table(nrows, pos_step, coarse, onehots):
        n = np.arange(nrows, dtype=np.int32)
        pos = n * pos_step
        hi = (pos // coarse) * coarse
        lo = pos - hi
        cols = [hi, hi, hi, lo, lo, lo] + [np.zeros_like(n)] * (FEAT_ONEHOT - 6)
        if onehots:
            cols += [(n // SEL_BLOCK == b).astype(np.int32) for b in range(SLC_CHUNK // SEL_BLOCK)]
        cols += [np.zeros_like(n)] * (HEAD_DIM - len(cols))
        return jnp.asarray(np.stack(cols, axis=1).astype(np.float32)).astype(BF16)

    slc = table(SLC_CHUNK, 1, 16, True)
    win = table(WINDOW + Q_TILE, 1, 16, False)
    cmp_ = table(min(CMP_CHUNK, nc), CMP_STRIDE, 256, False)
    return slc, win, cmp_


def _nsa_call(qt, proj, kc, vct, vst, vwt, gates, slopes, qfeat, *, t, ks_col, kw_col):
    nq = t // Q_TILE
    n_sel = t // SEL_BLOCK
    nc = t // CMP_STRIDE
    cc = min(CMP_CHUNK, nc)
    top_n = min(SEL_TOP_N, n_sel)
    hg = HEADS_PER_GROUP
    wide = hg * Q_TILE
    sel_per_chunk = SLC_CHUNK // SEL_BLOCK
    win_span = WINDOW + Q_TILE
    kdim = 2 * HEAD_DIM
    tab_slc, tab_win, tab_cmp = _nsa_feature_tables(nc)

    def body(q_ref, kc_ref, vct_ref, ks_ref, vst_ref, kw_ref, vwt_ref, g_ref, sl_ref, qf_ref,
             ts_ref, tw_ref, tc_ref, o_ref,
             qt_ref, kss_ref, ksw_ref, ksc_ref, m_ref, acc_ref, p_ref, mw_ref, accw_ref, pw_ref,
             out_ref, s_ref, ps_ref, score_ref, sel_ref, flag_ref):
        g = pl.program_id(0)
        i = pl.program_id(1)
        t0 = i * Q_TILE
        tq = t0 + lax.broadcasted_iota(jnp.int32, (1, Q_TILE), 1)

        for h in range(hg):
            qt_ref[0:HEAD_DIM, h * Q_TILE:(h + 1) * Q_TILE] = q_ref[h * HEAD_DIM:(h + 1) * HEAD_DIM, :]
        qt_ref[HEAD_DIM:HEAD_DIM + FEAT_ONEHOT, :] = qf_ref[...]
        qt_ref[HEAD_DIM + FEAT_ONEHOT:kdim, :] = jnp.zeros((HEAD_DIM - FEAT_ONEHOT, wide), BF16)
        kss_ref[:, HEAD_DIM:kdim] = ts_ref[...]
        ksw_ref[:, HEAD_DIM:kdim] = tw_ref[...]
        ksc_ref[:, HEAD_DIM:kdim] = tc_ref[...]

        state_main = (m_ref, acc_ref, p_ref)
        state_win = (mw_ref, accw_ref, pw_ref)

        def reset(state):
            state[0][...] = jnp.full((1, wide), NEG_INF, F32)
            state[1][...] = jnp.zeros((V_ROWS, wide), F32)

        def chunk_offset(ref_pos):
            return sl_ref[...] * (ref_pos - t0).astype(F32)

        def row_blocks(nrows):
            return [(r, min(ROW_BLOCK, nrows - r)) for r in range(0, nrows, ROW_BLOCK)]

        def scores(kst_ref, nrows):
            return [jnp.dot(kst_ref[r:r + n, :], qt_ref[...], preferred_element_type=F32)
                    for r, n in row_blocks(nrows)]

        def softmax_pv(state, sts, vt_c, off, nrows, mask=None):
            m_ref, acc_ref, p_ref = state
            for (r, n), st in zip(row_blocks(nrows), sts):
                alphas = []
                keep = None if mask is None else mask(r, n)
                for h in range(hg):
                    sl = slice(h * Q_TILE, (h + 1) * Q_TILE)
                    s = st[:, sl]
                    if keep is not None:
                        s = jnp.where(keep, s, NEG_INF)
                    o = off[:, sl]
                    m_old = m_ref[:, sl]
                    m_new = jnp.maximum(m_old, jnp.max(s, axis=0, keepdims=True) + o)
                    alphas.append(jnp.exp2(m_old - m_new))
                    m_ref[:, sl] = m_new
                    p_ref[r:r + n, sl] = jnp.exp2(s - (m_new - o)).astype(BF16)
                pv = jnp.dot(vt_c[:, r:r + n], p_ref[r:r + n, :], preferred_element_type=F32)
                acc_ref[...] = acc_ref[...] * jnp.concatenate(alphas, axis=1) + pv

        def attend(state, kst_ref, vt_c, off, nrows, mask=None):
            softmax_pv(state, scores(kst_ref, nrows), vt_c, off, nrows, mask)

        def inv_norm(state, sl):
            valid = state[0][:, sl] > 0.5 * NEG_INF
            denom = state[1][HEAD_DIM:HEAD_DIM + 1, sl]
            return jnp.where(valid, 1.0 / jnp.maximum(denom, 1e-30), 0.0)

        def finish(state, branch, first):
            for h in range(hg):
                sl = slice(h * Q_TILE, (h + 1) * Q_TILE)
                gate = g_ref[pl.ds(branch * N_HEADS + g * hg + h, 1), :]
                contrib = state[1][0:HEAD_DIM, sl] * (inv_norm(state, sl) * gate)
                if first:
                    out_ref[:, sl] = contrib
                else:
                    out_ref[:, sl] += contrib

        reset(state_main)
        ps_ref[...] = jnp.zeros(ps_ref.shape, F32)
        n_cmp_keys = (t0 + Q_TILE - CMP_BLOCK) // CMP_STRIDE + 1
        n_cc = (n_cmp_keys + cc - 1) // cc

        def cmp_scores(c, carry):
            r0 = pl.multiple_of(c * cc, cc)
            n = r0 + lax.broadcasted_iota(jnp.int32, (cc, Q_TILE), 0)
            mask = (n * CMP_STRIDE + (CMP_BLOCK - 1)) <= tq
            off = chunk_offset(r0 * CMP_STRIDE + (CMP_BLOCK - 1))
            ksc_ref[:, 0:HEAD_DIM] = kc_ref[pl.ds(r0, cc), :]
            for hp in range(hg // 2):
                pair = slice(hp * 2 * Q_TILE, (hp + 1) * 2 * Q_TILE)
                st = jnp.dot(ksc_ref[...], qt_ref[:, pair], preferred_element_type=F32)
                for hh in range(2):
                    sl = slice((2 * hp + hh) * Q_TILE, (2 * hp + hh + 1) * Q_TILE)
                    s = jnp.where(mask, st[:, hh * Q_TILE:(hh + 1) * Q_TILE] + off[:, sl], NEG_INF)
                    s_ref[pl.ds(r0, cc), sl] = s
                    m_ref[:, sl] = jnp.maximum(m_ref[:, sl], jnp.max(s, axis=0, keepdims=True))
            return carry

        lax.fori_loop(0, n_cc, cmp_scores, 0)

        def cmp_probs(c, carry):
            r0 = pl.multiple_of(c * cc, cc)
            for hp in range(hg // 2):
                pair = slice(hp * 2 * Q_TILE, (hp + 1) * 2 * Q_TILE)
                for hh in range(2):
                    sl = slice((2 * hp + hh) * Q_TILE, (2 * hp + hh + 1) * Q_TILE)
                    p = jnp.exp2(s_ref[pl.ds(r0, cc), sl] - m_ref[:, sl])
                    s_ref[pl.ds(r0, cc), sl] = p
                    p_ref[0:cc, sl] = p.astype(BF16)
                acc_ref[:, pair] += jnp.dot(vct_ref[c], p_ref[0:cc, pair], preferred_element_type=F32)
            return carry

        lax.fori_loop(0, n_cc, cmp_probs, 0)
        finish(state_main, 0, True)

        def ps_body(c, carry):
            r0 = pl.multiple_of(c * cc, cc)
            tot = jnp.zeros((cc, Q_TILE), F32)
            for h in range(hg):
                sl = slice(h * Q_TILE, (h + 1) * Q_TILE)
                tot = tot + s_ref[pl.ds(r0, cc), sl] * inv_norm(state_main, sl)
            ps_ref[pl.ds(8 + r0, cc), :] = tot
            return carry

        lax.fori_loop(0, n_cc, ps_body, 0)

        ratio = SEL_BLOCK // CMP_STRIDE
        imp = ps_ref[pl.ds(7, n_sel, stride=ratio), :]
        for kk in range(ratio):
            imp = imp + ps_ref[pl.ds(8 + kk, n_sel, stride=ratio), :]
        jrow = lax.broadcasted_iota(jnp.int32, (n_sel, Q_TILE), 0)
        cur = tq // SEL_BLOCK
        forced = jnp.where(jrow == 0, 1.0, jnp.where(jrow == cur, 1.0, jnp.where(jrow == cur - 1, 1.0, 0.0)))
        score_ref[...] = jnp.where(forced > 0.5, -2.0, jnp.where(jrow <= cur, imp, -1.0))
        sel_ref[...] = forced
        jrow_f = jrow.astype(F32)

        def run_topk(nrows):
            jr = jrow_f[0:nrows, :]

            def topk_body(r, carry):
                sc = score_ref[0:nrows, :]
                mx = jnp.max(sc, axis=0, keepdims=True)
                first = jnp.min(jnp.where(sc == mx, jr, float(n_sel)), axis=0, keepdims=True)
                pick = jr == first
                sel_ref[0:nrows, :] = jnp.where(pick, 1.0, sel_ref[0:nrows, :])
                score_ref[0:nrows, :] = jnp.where(pick, -2.0, sc)
                return carry

            lax.fori_loop(0, top_n - N_FORCED, topk_body, 0)

        half_rows = n_sel // 2
        tile_in_first_half = (t0 + Q_TILE - 1) // SEL_BLOCK < half_rows

        @pl.when(tile_in_first_half)
        def _():
            run_topk(half_rows)

        @pl.when(jnp.logical_not(tile_in_first_half))
        def _():
            run_topk(n_sel)
        for c in range(t // SLC_CHUNK):
            blocks = sel_ref[c * sel_per_chunk:(c + 1) * sel_per_chunk, :]
            flag_ref[c] = (jnp.max(blocks) > 0.5).astype(jnp.int32)

        reset(state_main)
        last_sc = (t0 + Q_TILE - 1) // SLC_CHUNK

        def slc_stage(c):
            r0 = pl.multiple_of(c * SLC_CHUNK, SLC_CHUNK)
            kss_ref[:, 0:HEAD_DIM] = ks_ref[pl.ds(r0, SLC_CHUNK), :]
            sel8 = sel_ref[pl.ds(pl.multiple_of(c * sel_per_chunk, sel_per_chunk), sel_per_chunk), :]
            bias = (sel8 - 1.0) * MASK_BIG
            rows = jnp.concatenate([jnp.concatenate([bias] * hg, axis=1),
                                    jnp.zeros((FEAT_ONEHOT - sel_per_chunk, wide), F32)], axis=0)
            qt_ref[HEAD_DIM + FEAT_ONEHOT:HEAD_DIM + 2 * FEAT_ONEHOT, :] = rows.astype(BF16)
            return r0

        def slc_body(c, carry):
            @pl.when(flag_ref[c] > 0)
            def _():
                r0 = slc_stage(c)
                attend(state_main, kss_ref, vst_ref[c], chunk_offset(r0), SLC_CHUNK)

            return carry

        lax.fori_loop(0, last_sc, slc_body, 0)

        reset(state_win)
        r0s = slc_stage(last_sc)
        n_wc = win_span // WIN_CHUNK
        c0 = jnp.maximum(i * (Q_TILE // WIN_CHUNK) - WINDOW // WIN_CHUNK, 0)
        r0w = pl.multiple_of(c0 * WIN_CHUNK, WIN_CHUNK)
        ksw_ref[:, 0:HEAD_DIM] = kw_ref[pl.ds(r0w, win_span), :]
        sts_slc = scores(kss_ref, SLC_CHUNK)
        sts_win = scores(ksw_ref, win_span)

        def masks(r, n):
            return (r0s + r + lax.broadcasted_iota(jnp.int32, (n, Q_TILE), 0)) <= tq

        softmax_pv(state_main, sts_slc, vst_ref[last_sc], chunk_offset(r0s), SLC_CHUNK, masks)
        finish(state_main, 1, False)

        def maskw(r, n):
            dpos = (r0w + r + lax.broadcasted_iota(jnp.int32, (n, Q_TILE), 0)) - tq
            return jnp.where(dpos <= 0, dpos, -2 * WINDOW) > -WINDOW

        vtw = jnp.concatenate([vwt_ref[c0 + kk] for kk in range(n_wc)], axis=1)
        softmax_pv(state_win, sts_win, vtw, chunk_offset(r0w), win_span, maskw)
        finish(state_win, 2, False)

        for h in range(hg):
            sl = slice(h * Q_TILE, (h + 1) * Q_TILE)
            o_ref[:, h * HEAD_DIM:(h + 1) * HEAD_DIM] = out_ref[:, sl].T.astype(o_ref.dtype)

    one = pl.Buffered(1)
    in_specs = [
        pl.BlockSpec((hg * HEAD_DIM, Q_TILE), lambda g, i: (g, i)),
        pl.BlockSpec((None, nc, HEAD_DIM), lambda g, i: (g, 0, 0)),
        pl.BlockSpec((None, nc // cc, V_ROWS, cc), lambda g, i: (g, 0, 0, 0)),
        pl.BlockSpec((t, HEAD_DIM), lambda g, i: (0, ks_col + g), pipeline_mode=one),
        pl.BlockSpec((t // SLC_CHUNK, V_ROWS, SLC_CHUNK), lambda g, i: (0, g, 0), pipeline_mode=one),
        pl.BlockSpec((t, HEAD_DIM), lambda g, i: (0, kw_col + g), pipeline_mode=one),
        pl.BlockSpec((t // WIN_CHUNK, V_ROWS, WIN_CHUNK), lambda g, i: (0, g, 0), pipeline_mode=one),
        pl.BlockSpec((gates.shape[0], Q_TILE), lambda g, i: (0, i)),
        pl.BlockSpec((None, 1, wide), lambda g, i: (g, 0, 0)),
        pl.BlockSpec((None, FEAT_ONEHOT, wide), lambda g, i: (g, 0, 0)),
        pl.BlockSpec(tab_slc.shape, lambda g, i: (0, 0)),
        pl.BlockSpec(tab_win.shape, lambda g, i: (0, 0)),
        pl.BlockSpec(tab_cmp.shape, lambda g, i: (0, 0)),
    ]
    scratch = [
        pltpu.VMEM((kdim, wide), BF16),
        pltpu.VMEM((SLC_CHUNK, kdim), BF16),
        pltpu.VMEM((win_span, kdim), BF16),
        pltpu.VMEM((cc, kdim), BF16),
        pltpu.VMEM((1, wide), F32),
        pltpu.VMEM((V_ROWS, wide), F32),
        pltpu.VMEM((SLC_CHUNK, wide), BF16),
        pltpu.VMEM((1, wide), F32),
        pltpu.VMEM((V_ROWS, wide), F32),
        pltpu.VMEM((win_span, wide), BF16),
        pltpu.VMEM((HEAD_DIM, wide), F32),
        pltpu.VMEM((nc, wide), F32),
        pltpu.VMEM((8 + nc, Q_TILE), F32),
        pltpu.VMEM((n_sel, Q_TILE), F32),
        pltpu.VMEM((n_sel, Q_TILE), F32),
        pltpu.SMEM((t // SLC_CHUNK,), jnp.int32),
    ]
    return pl.pallas_call(
        body,
        grid=(N_GROUPS, nq),
        in_specs=in_specs,
        out_specs=pl.BlockSpec((Q_TILE, hg * HEAD_DIM), lambda g, i: (i, g)),
        out_shape=jax.ShapeDtypeStruct((t, N_HEADS * HEAD_DIM), BF16),
        scratch_shapes=scratch,
        compiler_params=_cparams("parallel", "parallel"),
        name="nsa",
    )(qt, kc, vct, proj, vst, proj, vwt, gates, slopes, qfeat, tab_slc, tab_win, tab_cmp)


def _glu_proj_call(h, w_glu):
    t, d = h.shape
    ch = w_glu.shape[1] // 2
    tm = min(1024, t)
    tn = 512
    nj = ch // tn

    def body(h_ref, wa_ref, wg_ref, o_ref):
        hv = h_ref[...]
        a = jnp.dot(hv, wa_ref[...], preferred_element_type=F32)
        g = jnp.dot(hv, wg_ref[...], preferred_element_type=F32)
        o_ref[...] = (a * _sigmoid(g)).astype(o_ref.dtype)

    return pl.pallas_call(
        body,
        grid=(t // tm, nj),
        in_specs=[pl.BlockSpec((tm, d), lambda i, j: (i, 0)),
                  pl.BlockSpec((d, tn), lambda i, j: (0, j)),
                  pl.BlockSpec((d, tn), lambda i, j: (0, nj + j))],
        out_specs=pl.BlockSpec((tm, tn), lambda i, j: (i, j)),
        out_shape=jax.ShapeDtypeStruct((t, ch), BF16),
        compiler_params=_cparams("parallel", "parallel"),
        name="proj_glu",
    )(h, w_glu, w_glu)


def _conv_front_call(u, w_dw, b_dw, ln_g, ln_b):
    t = u.shape[0]
    ch = w_dw.shape[1]
    tm = min(256, t)
    halo = 32
    rb = 32
    strip = 512
    w_rep = jnp.repeat(w_dw, F32_SUBLANES, axis=0)

    def body(a_ref, ah_ref, w_ref, b_ref, lg_ref, lb_ref, o_ref, u_ref, sb_ref, y_ref):
        i = pl.program_id(0)
        u_ref[halo:halo + tm, :] = a_ref[...].astype(F32)
        u_ref[0:halo, :] = jnp.where(i > 0, ah_ref[...].astype(F32), 0.0)
        for cs in range(ch // strip):
            cols = slice(cs * strip, (cs + 1) * strip)
            ue = u_ref[:, cols]
            sb_ref[0] = ue
            for part in range(1, F32_SUBLANES):
                sb_ref[part] = pltpu.roll(ue, part, 0)

            def rows(r, carry, cols=cols):
                r0 = pl.multiple_of(r * rb, rb)
                acc = jnp.zeros((rb // F32_SUBLANES, F32_SUBLANES, strip), F32)
                for s in range(CONV_WIDTH):
                    whole, part = divmod(s, F32_SUBLANES)
                    k = CONV_WIDTH - 1 - s
                    tap = sb_ref[part, pl.ds(halo + r0 - F32_SUBLANES * whole, rb), :]
                    wk = w_ref[k * F32_SUBLANES:(k + 1) * F32_SUBLANES, cols]
                    acc = acc + tap.reshape(rb // F32_SUBLANES, F32_SUBLANES, strip) * wk
                y_ref[pl.ds(r0, rb), cols] = acc.reshape(rb, strip) + b_ref[:, cols]
                return carry

            lax.fori_loop(0, tm // rb, rows, 0)
        y = y_ref[...]
        mu = jnp.mean(y, axis=-1, keepdims=True)
        yc = y - mu
        var = jnp.mean(yc * yc, axis=-1, keepdims=True)
        z = yc * lax.rsqrt(var + EPS) * lg_ref[...] + lb_ref[...]
        o_ref[...] = (z * _sigmoid(z)).astype(o_ref.dtype)

    per = tm // halo
    return pl.pallas_call(
        body,
        grid=(t // tm,),
        in_specs=[pl.BlockSpec((tm, ch), lambda i: (i, 0)),
                  pl.BlockSpec((halo, ch), lambda i: (jnp.maximum(i * per - 1, 0), 0)),
                  pl.BlockSpec((CONV_WIDTH * F32_SUBLANES, ch), lambda i: (0, 0)),
                  pl.BlockSpec((1, ch), lambda i: (0, 0)),
                  pl.BlockSpec((1, ch), lambda i: (0, 0)),
                  pl.BlockSpec((1, ch), lambda i: (0, 0))],
        out_specs=pl.BlockSpec((tm, ch), lambda i: (i, 0)),
        out_shape=jax.ShapeDtypeStruct((t, ch), BF16),
        scratch_shapes=[pltpu.VMEM((halo + tm, ch), F32),
                        pltpu.VMEM((F32_SUBLANES, halo + tm, strip), F32),
                        pltpu.VMEM((tm, ch), F32)],
        compiler_params=_cparams("parallel"),
        name="conv_front",
    )(u, u, w_rep, b_dw.reshape(1, ch), ln_g.reshape(1, ch), ln_b.reshape(1, ch))


def _ffn_up_call(h2, w_up, w_dw, b_dw):
    t, d = h2.shape
    n2 = w_up.shape[1]
    f = n2 // 2
    tm = min(1024, t)
    rsub = 256
    tn = 512
    nj = f // tn
    sub = F32_SUBLANES
    w_pad = jnp.concatenate([w_dw, jnp.zeros((sub - FFN_CONV_WIDTH, n2), F32)], axis=0)
    b2 = b_dw.reshape(1, n2)

    def body(h_ref, wa_ref, wg_ref, da_ref, dg_ref, ba_ref, bg_ref, o_ref, ca_ref, cg_ref):
        i = pl.program_id(0)
        j = pl.program_id(1)

        @pl.when(i == 0)
        def _():
            ca_ref[j] = jnp.zeros((sub, tn), F32)
            cg_ref[j] = jnp.zeros((sub, tn), F32)

        wa = wa_ref[...]
        wg = wg_ref[...]

        def conv(up, prev, d_ref, b_ref):
            ext = jnp.concatenate([prev, up], axis=0)
            acc = b_ref[...] + up * d_ref[FFN_CONV_WIDTH - 1:FFN_CONV_WIDTH, :]
            for s in range(1, FFN_CONV_WIDTH):
                k = FFN_CONV_WIDTH - 1 - s
                acc = acc + pltpu.roll(ext, s, 0)[sub:, :] * d_ref[k:k + 1, :]
            return acc

        def project(rs):
            hv = h_ref[rs * rsub:(rs + 1) * rsub, :]
            return (jnp.dot(hv, wa, preferred_element_type=F32),
                    jnp.dot(hv, wg, preferred_element_type=F32))

        prev_a = ca_ref[j]
        prev_g = cg_ref[j]
        n_rs = tm // rsub
        ups = project(0)
        for rs in range(n_rs):
            rows = slice(rs * rsub, (rs + 1) * rsub)
            up_a, up_g = ups
            if rs + 1 < n_rs:
                ups = project(rs + 1)
            ca = conv(up_a, prev_a, da_ref, ba_ref)
            cg = conv(up_g, prev_g, dg_ref, bg_ref)
            o_ref[rows, :] = (cg * _sigmoid(cg) * ca).astype(o_ref.dtype)
            prev_a = up_a[rsub - sub:, :]
            prev_g = up_g[rsub - sub:, :]
        ca_ref[j] = prev_a
        cg_ref[j] = prev_g

    return pl.pallas_call(
        body,
        grid=(t // tm, nj),
        in_specs=[pl.BlockSpec((tm, d), lambda i, j: (i, 0)),
                  pl.BlockSpec((d, tn), lambda i, j: (0, j)),
                  pl.BlockSpec((d, tn), lambda i, j: (0, nj + j)),
                  pl.BlockSpec((sub, tn), lambda i, j: (0, j)),
                  pl.BlockSpec((sub, tn), lambda i, j: (0, nj + j)),
                  pl.BlockSpec((1, tn), lambda i, j: (0, j)),
                  pl.BlockSpec((1, tn), lambda i, j: (0, nj + j))],
        out_specs=pl.BlockSpec((tm, tn), lambda i, j: (i, j)),
        out_shape=jax.ShapeDtypeStruct((t, f), BF16),
        scratch_shapes=[pltpu.VMEM((nj, sub, tn), F32), pltpu.VMEM((nj, sub, tn), F32)],
        compiler_params=_cparams("arbitrary", "arbitrary"),
        name="ffn_up",
    )(h2, w_up, w_up, w_pad, w_pad, b2, b2)


def _layer(x, c_row, p):
    t, d = x.shape
    ada = _ada_call(c_row, p["w_ada"], p["b_ada"])

    w_in = p["w_in"]
    qw = N_HEADS * HEAD_DIM
    kvw = N_GROUPS * HEAD_DIM
    o = [0, qw]
    for _ in range(6):
        o.append(o[-1] + kvw)
    o.append(o[-1] + 3 * N_HEADS)
    o.append(o[-1] + 2 * d)
    o.append(o[-1] + 2 * d)
    w_q, w_kc, w_vc, w_ks, w_vs, w_kw, w_vw, w_gn, w_glu, w_mg = [w_in[:, o[k]:o[k + 1]] for k in range(10)]
    w_rm = jnp.concatenate([w_kc, w_vc, w_ks, w_kw, w_mg], axis=1).astype(BF16)
    w_glu = w_glu.astype(BF16)
    w_t = _transpose_cast_call(w_in, [(o[0], qw), (o[4], kvw), (o[6], kvw), (o[7], 256)])
    w_qt = w_t[0:qw]
    w_vt = w_t[qw:qw + 2 * kvw]
    n_gate_rows = 64
    w_gt = w_t[qw + 2 * kvw:qw + 2 * kvw + n_gate_rows]

    h1 = _norm_call(x, p["norm1_g"], ada, shift_idx=0, scale_idx=1)
    proj = _matmul_call(h1, w_rm, tm=1024, tn=1024, out_dtype=BF16, name="proj_rm")
    u = _glu_proj_call(h1, w_glu)
    qt, vst, vwt, gates = _nt_proj_call(h1, w_qt, w_vt, w_gt, HEAD_DIM ** -0.5 * LOG2E)

    nc = t // CMP_STRIDE
    cc = min(CMP_CHUNK, nc)
    half = CMP_BLOCK // CMP_STRIDE
    assert half == 2

    def expand(w1):
        w1r = w1.reshape(half, CMP_STRIDE, HEAD_DIM, CMP_HIDDEN)
        outs = []
        for hf in range(half):
            per_g = []
            for g in range(N_GROUPS):
                z = jnp.zeros((CMP_STRIDE, N_GROUPS, HEAD_DIM, CMP_HIDDEN), F32).at[:, g].set(w1r[hf])
                per_g.append(z.reshape(CMP_STRIDE * kvw, CMP_HIDDEN))
            outs.append(jnp.stack(per_g).astype(BF16))
        return outs

    pe = p["cmp_pe"].reshape(half, CMP_STRIDE, 1, HEAD_DIM)
    pe_rows = jnp.broadcast_to(pe, (half, CMP_STRIDE, N_GROUPS, HEAD_DIM)).reshape(half, CMP_STRIDE * kvw)
    pe8 = jnp.concatenate([pe_rows, jnp.zeros((8 - half, CMP_STRIDE * kvw), F32)], axis=0).astype(BF16)
    kc_raw = proj[:, 0:kvw].reshape(nc, CMP_STRIDE * kvw)
    vc_raw = proj[:, kvw:2 * kvw].reshape(nc, CMP_STRIDE * kvw)
    wka, wkb = expand(p["w_kc1"])
    wva, wvb = expand(p["w_vc1"])
    kc = _compress_call(kc_raw, wka, wkb, pe8, p["w_kc2"].astype(BF16), transposed=False, cc=cc)
    vct = _compress_call(vc_raw, wva, wvb, pe8, p["w_vc2"].T.astype(BF16), transposed=True, cc=cc)

    head_ids = np.arange(1, N_HEADS + 1, dtype=np.float64)
    slopes_np = (np.exp2(-8.0 * head_ids / N_HEADS) * LOG2E).astype(np.float32)
    slopes_np = np.broadcast_to(slopes_np.reshape(N_GROUPS, 1, HEADS_PER_GROUP, 1),
                                (N_GROUPS, 1, HEADS_PER_GROUP, Q_TILE)).reshape(N_GROUPS, 1, HEADS_PER_GROUP * Q_TILE)
    s1 = slopes_np.astype(BF16)
    s2 = (slopes_np - s1.astype(np.float32)).astype(BF16)
    s3 = (slopes_np - s1.astype(np.float32) - s2.astype(np.float32)).astype(BF16)
    zero_rows = np.zeros((N_GROUPS, FEAT_ONEHOT - 6, slopes_np.shape[-1]), dtype=s1.dtype)
    qfeat = jnp.asarray(np.concatenate([s1, s2, s3, s1, s2, s3, zero_rows], axis=1))
    slopes = jnp.asarray(slopes_np)
    o_nsa = _nsa_call(qt, proj, kc, vct, vst, vwt, gates, slopes, qfeat, t=t,
                      ks_col=(2 * kvw) // HEAD_DIM, kw_col=(3 * kvw) // HEAD_DIM)

    u2 = _conv_front_call(u, p["conv_dw_w"], p["conv_dw_b"], p["conv_ln_g"], p["conv_ln_b"])

    merged = _merge_call(o_nsa, p["w_o_nsa"].astype(BF16), u2, p["conv_pw_w"].astype(BF16), p["conv_pw_b"],
                         proj, gate_col=4 * kvw)
    x1, h2 = _out_norm_call(merged, p["w_out"].astype(BF16), x, ada, p["norm2_g"],
                            gate_idx=2, shift_idx=3, scale_idx=4)
    act = _ffn_up_call(h2, p["ffn_w_up"].astype(BF16), p["ffn_dw_w"], p["ffn_dw_b"])
    tm_down = min(1024, t)
    x2 = _matmul_call(
        act, p["ffn_w_down"].astype(BF16), tm=tm_down, tn=512, out_dtype=F32, name="ffn_down",
        epilogue=lambda acc, x_ref, g_ref: x_ref[...] + g_ref[0:1, :] * acc,
        extras=(x1, ada),
        extra_specs=(pl.BlockSpec((tm_down, 512), lambda i, j: (i, j)),
                     pl.BlockSpec((8, 512), lambda i, j: (0, 5 * (d // 512) + j))))
    return x2


def kernel(x, c, w_ada, b_ada, norm1_g, w_in, cmp_pe, w_kc1, w_kc2, w_vc1, w_vc2, w_o_nsa, conv_dw_w, conv_dw_b, conv_ln_g, conv_ln_b, conv_pw_w, conv_pw_b, w_out, norm2_g, ffn_w_up, ffn_dw_w, ffn_dw_b, ffn_w_down, final_g):
    b, t, d = x.shape
    stacked = dict(w_ada=w_ada, b_ada=b_ada, norm1_g=norm1_g, w_in=w_in, cmp_pe=cmp_pe, w_kc1=w_kc1,
                   w_kc2=w_kc2, w_vc1=w_vc1, w_vc2=w_vc2, w_o_nsa=w_o_nsa, conv_dw_w=conv_dw_w,
                   conv_dw_b=conv_dw_b, conv_ln_g=conv_ln_g, conv_ln_b=conv_ln_b, conv_pw_w=conv_pw_w,
                   conv_pw_b=conv_pw_b, w_out=w_out, norm2_g=norm2_g, ffn_w_up=ffn_w_up,
                   ffn_dw_w=ffn_dw_w, ffn_dw_b=ffn_dw_b, ffn_w_down=ffn_w_down)
    depth = w_ada.shape[0]
    xs = x.reshape(b * t, d)
    outs = []
    for bi in range(b):
        xb = xs if b == 1 else xs[bi * t:(bi + 1) * t]
        for layer in range(depth):
            xb = _layer(xb, c[bi:bi + 1], {k: v[layer] for k, v in stacked.items()})
        outs.append(_norm_call(xb, final_g, out_dtype=x.dtype))
    out = outs[0] if b == 1 else jnp.concatenate(outs, axis=0)
    return out.reshape(b, t, d)
```
